```python
import math
import jax
import jax.numpy as jnp
from jax import lax
import numpy as np

D_MODEL = 1024
BATCH = 8
SEQ = 4096
DEPTH = 4

FOX_HEADS = 8
FOX_HEAD_DIM = D_MODEL // 16
FOX_WIDTH = FOX_HEADS * FOX_HEAD_DIM
FOX_Q_BLOCK = 128
POOL_WINDOWS = (2, 4, 8, 16)
POOL_GROUPS = len(POOL_WINDOWS)
POOL_WIDTH = D_MODEL // 2
POOL_GROUP_DIM = POOL_WIDTH // POOL_GROUPS
EVEN_IN = 3 * FOX_WIDTH + FOX_HEADS + POOL_WIDTH
MOBA_HEADS = 16
MOBA_HEAD_DIM = D_MODEL // MOBA_HEADS
MOBA_WIDTH = MOBA_HEADS * MOBA_HEAD_DIM
MOBA_BLOCK = 256
MOBA_TOPK = 3
MOBA_Q_CHUNK = 16
D_FF = D_MODEL * 7 // 2
N_EXPERTS = 8
TOP_K = 2
LN_EPS = 1e-5
DEEPNORM_ALPHA = (2 * DEPTH) ** 0.25
DEEPNORM_BETA = (8 * DEPTH) ** -0.25
N_EVEN = (DEPTH + 1) // 2
N_ODD = DEPTH // 2

kernel_name = 'fox_pool_moba_moe_deepnorm_trunk'


def layer_norm(x, g, b):
    xf = x.astype(jnp.float32)
    mu = xf.mean(-1, keepdims=True)
    var = jnp.square(xf - mu).mean(-1, keepdims=True)
    return ((xf - mu) * lax.rsqrt(var + LN_EPS) * g + b).astype(x.dtype)


def split_heads(t, n_heads):
    b, s, _ = t.shape
    return t.reshape(b, s, n_heads, -1).transpose(0, 2, 1, 3)


def merge_heads(t):
    b, h, s, d = t.shape
    return t.transpose(0, 2, 1, 3).reshape(b, s, h * d)


def fox_attention(q, k, v, logf):
    b, h, s, d = q.shape
    nq = s // FOX_Q_BLOCK
    scale = d ** -0.5
    cum = jnp.cumsum(logf, axis=-1)
    qb = q.reshape(b, h, nq, FOX_Q_BLOCK, d).transpose(2, 0, 1, 3, 4)
    cb = cum.reshape(b, h, nq, FOX_Q_BLOCK).transpose(2, 0, 1, 3)
    kpos = jnp.arange(s)

    def block(args):
        q_blk, c_blk, i = args
        logits = jnp.einsum('bhqd,bhkd->bhqk', q_blk, k).astype(jnp.float32) * scale
        logits = logits + c_blk[..., None] - cum[:, :, None, :]
        qpos = i * FOX_Q_BLOCK + jnp.arange(FOX_Q_BLOCK)
        logits = jnp.where(kpos[None, :] <= qpos[:, None], logits, -jnp.inf)
        p = jax.nn.softmax(logits, axis=-1)
        return jnp.einsum('bhqk,bhkd->bhqd', p.astype(v.dtype), v)

    out = lax.map(block, (qb, cb, jnp.arange(nq)))
    return out.transpose(1, 2, 0, 3, 4).reshape(b, h, s, d)


def multiscale_pool(u, pool_w, pool_scale):
    b, s, _ = u.shape
    ug = u.reshape(b, s, POOL_GROUPS, POOL_GROUP_DIM).astype(jnp.float32)
    csum = jnp.cumsum(ug, axis=1)
    t = jnp.arange(s)
    pooled = []
    for g, w in enumerate(POOL_WINDOWS):
        cpad = jnp.pad(csum[:, :, g], ((0, 0), (w, 0), (0, 0)))
        wsum = cpad[:, w:] - cpad[:, :s]
        cnt = jnp.minimum(t + 1, w).astype(jnp.float32)
        pooled.append(wsum / cnt[None, :, None])
    mixed = (jnp.stack(pooled, axis=2) - ug).astype(u.dtype)
    y = jnp.einsum('bsgc,gcd->bsgd', mixed, pool_w)
    return y.reshape(b, s, POOL_WIDTH) * pool_scale


def moba_attention(q, k, v):
    b, h, s, d = q.shape
    scale = d ** -0.5
    nb = -(-s // MOBA_BLOCK)
    pad = nb * MOBA_BLOCK - s
    kp = jnp.pad(k, ((0, 0), (0, 0), (0, pad), (0, 0)))
    vp = jnp.pad(v, ((0, 0), (0, 0), (0, pad), (0, 0)))
    kb = kp.reshape(b, h, nb, MOBA_BLOCK, d)
    vb = vp.reshape(b, h, nb, MOBA_BLOCK, d)
    kmean = kb.astype(jnp.float32).mean(axis=3)
    gate = jnp.einsum('bhsd,bhnd->bhsn', q.astype(jnp.float32), kmean)
    qblk = jnp.arange(s) // MOBA_BLOCK
    past = jnp.arange(nb)[None, :] < qblk[:, None]
    gate = jnp.where(past, gate, -jnp.inf)
    n_sel = min(MOBA_TOPK, nb)
    _, sel = lax.top_k(gate, n_sel)
    nc = s // MOBA_Q_CHUNK
    qc = q.reshape(b, h, nc, MOBA_Q_CHUNK, d).transpose(2, 0, 1, 3, 4)
    selc = sel.reshape(b, h, nc, MOBA_Q_CHUNK, n_sel).transpose(2, 0, 1, 3, 4)
    b_i = jnp.arange(b)[:, None, None, None]
    h_i = jnp.arange(h)[None, :, None, None]

    def chunk(args):
        q_c, sel_c, i = args
        start = i * MOBA_Q_CHUNK
        blk = start // MOBA_BLOCK
        k_sel = kb[b_i, h_i, sel_c]
        v_sel = vb[b_i, h_i, sel_c]
        s_sel = jnp.einsum('bhqd,bhqnkd->bhqnk', q_c, k_sel).astype(jnp.float32) * scale
        s_sel = jnp.where((sel_c < blk)[..., None], s_sel, -jnp.inf)
        s_sel = s_sel.reshape(b, h, MOBA_Q_CHUNK, n_sel * MOBA_BLOCK)
        k_own = lax.dynamic_slice_in_dim(kp, blk * MOBA_BLOCK, MOBA_BLOCK, axis=2)
        v_own = lax.dynamic_slice_in_dim(vp, blk * MOBA_BLOCK, MOBA_BLOCK, axis=2)
        s_own = jnp.einsum('bhqd,bhkd->bhqk', q_c, k_own).astype(jnp.float32) * scale
        kpos = blk * MOBA_BLOCK + jnp.arange(MOBA_BLOCK)
        qpos = start + jnp.arange(MOBA_Q_CHUNK)
        s_own = jnp.where(kpos[None, :] <= qpos[:, None], s_own, -jnp.inf)
        p = jax.nn.softmax(jnp.concatenate([s_sel, s_own], axis=-1), axis=-1).astype(v.dtype)
        p_sel = p[..., :n_sel * MOBA_BLOCK].reshape(b, h, MOBA_Q_CHUNK, n_sel, MOBA_BLOCK)
        p_own = p[..., n_sel * MOBA_BLOCK:]
        return (jnp.einsum('bhqnk,bhqnkd->bhqd', p_sel, v_sel)
                + jnp.einsum('bhqk,bhkd->bhqd', p_own, v_own))

    out = lax.map(chunk, (qc, selc, jnp.arange(nc)))
    return out.transpose(1, 2, 0, 3, 4).reshape(b, h, s, d)


def even_mixer(x, w_in, b_forget, pool_w, pool_scale, w_out):
    hcat = jnp.einsum('bsd,de->bse', x, w_in)
    q, k, v, f, u = jnp.split(hcat, [FOX_WIDTH, 2 * FOX_WIDTH, 3 * FOX_WIDTH, 3 * FOX_WIDTH + FOX_HEADS], axis=-1)
    logf = jax.nn.log_sigmoid((f + b_forget).astype(jnp.float32)).transpose(0, 2, 1)
    a = merge_heads(fox_attention(split_heads(q, FOX_HEADS), split_heads(k, FOX_HEADS),
                                  split_heads(v, FOX_HEADS), logf))
    p = multiscale_pool(u, pool_w, pool_scale)
    return jnp.einsum('bse,ed->bsd', jnp.concatenate([a, p], axis=-1), w_out)


def odd_mixer(x, w_in, w_out):
    hcat = jnp.einsum('bsd,de->bse', x, w_in)
    q, k, v = jnp.split(hcat, 3, axis=-1)
    o = moba_attention(split_heads(q, MOBA_HEADS), split_heads(k, MOBA_HEADS), split_heads(v, MOBA_HEADS))
    return jnp.einsum('bse,ed->bsd', merge_heads(o), w_out)


def swiglu(x, w_gate, w_up, w_down):
    hid = jax.nn.silu(jnp.einsum('bsd,df->bsf', x, w_gate)) * jnp.einsum('bsd,df->bsf', x, w_up)
    return jnp.einsum('bsf,fd->bsd', hid, w_down)


def moe_ffn(x, w_router, w_gate, w_up, w_down):
    logits = jnp.einsum('bsd,de->bse', x, w_router).astype(jnp.float32)
    vals, idx = lax.top_k(logits, TOP_K)
    gates = jax.nn.softmax(vals, axis=-1)
    combine = jnp.sum(gates[..., None] * jax.nn.one_hot(idx, N_EXPERTS, dtype=jnp.float32), axis=-2)
    y = jnp.zeros_like(x)
    for e in range(N_EXPERTS):
        y = y + combine[..., e:e + 1].astype(x.dtype) * swiglu(x, w_gate[e], w_up[e], w_down[e])
    return y


def setup_inputs(seed: int = 0) -> dict:
    key = jax.random.key(seed)
    ks = jax.random.split(key, 20)
    nrm = jax.random.normal
    f32 = jnp.float32
    sd = D_MODEL ** -0.5
    sf = D_FF ** -0.5
    return {
        'x': nrm(ks[0], (BATCH, SEQ, D_MODEL), f32),
        'ln1_g': 1.0 + 0.02 * nrm(ks[1], (DEPTH, D_MODEL), f32),
        'ln1_b': 0.02 * nrm(ks[2], (DEPTH, D_MODEL), f32),
        'ln2_g': 1.0 + 0.02 * nrm(ks[3], (DEPTH, D_MODEL), f32),
        'ln2_b': 0.02 * nrm(ks[4], (DEPTH, D_MODEL), f32),
        'ev_w_in': sd * nrm(ks[5], (N_EVEN, D_MODEL, EVEN_IN), f32),
        'ev_b_forget': jnp.linspace(1.0, 4.0, FOX_HEADS, dtype=f32)[None, :] + 0.1 * nrm(ks[6], (N_EVEN, FOX_HEADS), f32),
        'ev_pool_w': POOL_GROUP_DIM ** -0.5 * nrm(ks[7], (N_EVEN, POOL_GROUPS, POOL_GROUP_DIM, POOL_GROUP_DIM), f32),
        'ev_pool_scale': 1.0 + 0.1 * nrm(ks[8], (N_EVEN, POOL_WIDTH), f32),
        'ev_w_out': DEEPNORM_BETA * sd * nrm(ks[9], (N_EVEN, FOX_WIDTH + POOL_WIDTH, D_MODEL), f32),
        'ev_ffn_gate': sd * nrm(ks[10], (N_EVEN, D_MODEL, D_FF), f32),
        'ev_ffn_up': sd * nrm(ks[11], (N_EVEN, D_MODEL, D_FF), f32),
        'ev_ffn_down': DEEPNORM_BETA * sf * nrm(ks[12], (N_EVEN, D_FF, D_MODEL), f32),
        'od_w_in': sd * nrm(ks[13], (N_ODD, D_MODEL, 3 * MOBA_WIDTH), f32),
        'od_w_out': DEEPNORM_BETA * MOBA_WIDTH ** -0.5 * nrm(ks[14], (N_ODD, MOBA_WIDTH, D_MODEL), f32),
        'od_router': sd * nrm(ks[15], (N_ODD, D_MODEL, N_EXPERTS), f32),
        'od_exp_gate': sd * nrm(ks[16], (N_ODD, N_EXPERTS, D_MODEL, D_FF), f32),
        'od_exp_up': sd * nrm(ks[17], (N_ODD, N_EXPERTS, D_MODEL, D_FF), f32),
        'od_exp_down': DEEPNORM_BETA * sf * nrm(ks[18], (N_ODD, N_EXPERTS, D_FF, D_MODEL), f32),
    }


def reference(x, ln1_g, ln1_b, ln2_g, ln2_b, ev_w_in, ev_b_forget, ev_pool_w, ev_pool_scale, ev_w_out,
              ev_ffn_gate, ev_ffn_up, ev_ffn_down, od_w_in, od_w_out, od_router, od_exp_gate, od_exp_up,
              od_exp_down):
    for layer in range(DEPTH):
        i = layer // 2
        if layer % 2 == 0:
            mix = even_mixer(x, ev_w_in[i], ev_b_forget[i], ev_pool_w[i], ev_pool_scale[i], ev_w_out[i])
        else:
            mix = odd_mixer(x, od_w_in[i], od_w_out[i])
        x = layer_norm(DEEPNORM_ALPHA * x + mix, ln1_g[layer], ln1_b[layer])
        if layer % 2 == 0:
            ffn = swiglu(x, ev_ffn_gate[i], ev_ffn_up[i], ev_ffn_down[i])
        else:
            ffn = moe_ffn(x, od_router[i], od_exp_gate[i], od_exp_up[i], od_exp_down[i])
        x = layer_norm(DEEPNORM_ALPHA * x + ffn, ln2_g[layer], ln2_b[layer])
    return x
```

```python
import functools

import jax
import jax.numpy as jnp
from jax import lax
from jax.experimental import pallas as pl
from jax.experimental.pallas import tpu as pltpu

F32 = jnp.float32
BF16 = jnp.bfloat16

D_MODEL = 1024
DEPTH = 4
HEAD_DIM = 64
FOX_HEADS = 8
FOX_WIDTH = FOX_HEADS * HEAD_DIM
POOL_WINDOWS = (2, 4, 8, 16)
POOL_WIDTH = 512
POOL_GROUP_DIM = 128
MOBA_HEADS = 16
MOBA_WIDTH = MOBA_HEADS * HEAD_DIM
MOBA_BLOCK = 256
MOBA_TOPK = 3
D_FF = 3584
N_EXPERTS = 8
LN_EPS = 1e-5
ALPHA = (2 * DEPTH) ** 0.25
QK_SCALE = HEAD_DIM ** -0.5

LANES = 128
ATT_TILE = 512
ROW_TILE = 512
FFN_ROW_TILE = 1024
FFN_COL_TILE = 512
CUM_BLOCK = 256
NEG_BIG = -1e30
VMEM_LIMIT = 48 * 1024 * 1024

_NT = (((1,), (1,)), ((), ()))


def _params(sem):
    return pltpu.CompilerParams(dimension_semantics=sem, vmem_limit_bytes=VMEM_LIMIT)


def _split3(x):
    hi = x.astype(BF16)
    r = x - hi.astype(F32)
    mid = r.astype(BF16)
    lo = (r - mid.astype(F32)).astype(BF16)
    return hi, mid, lo


def _ln(y, g, b):
    mu = jnp.mean(y, axis=-1, keepdims=True)
    yc = y - mu
    var = jnp.mean(yc * yc, axis=-1, keepdims=True)
    return yc * lax.rsqrt(var + LN_EPS) * g + b


def _proj_kernel(*refs, kinds):
    x_ref = refs[0]
    w_refs = refs[1:1 + len(kinds)]
    o_refs = refs[1 + len(kinds):]
    xb = x_ref[...].astype(BF16)
    for kind, w_ref, o_ref in zip(kinds, w_refs, o_refs):
        if kind == "nt":
            m = w_ref.shape[0]
            for c0 in range(0, m, 512):
                c1 = min(c0 + 512, m)
                o_ref[0, 0, c0:c1, :] = lax.dot_general(
                    w_ref[c0:c1, :], xb, _NT, preferred_element_type=F32).astype(o_ref.dtype)
        else:
            m = w_ref.shape[1]
            for c0 in range(0, m, 512):
                c1 = min(c0 + 512, m)
                o_ref[:, c0:c1] = jnp.dot(
                    xb, w_ref[:, c0:c1], preferred_element_type=F32).astype(o_ref.dtype)


def _proj(x2, seq, specs):
    n, d = x2.shape
    tm = ROW_TILE
    tps = seq // tm
    kinds = tuple(k for _, k, _ in specs)
    in_specs = [pl.BlockSpec((tm, d), lambda i: (i, 0))]
    out_specs, out_shapes = [], []
    for w, kind, dt in specs:
        in_specs.append(pl.BlockSpec(w.shape, lambda i: (0, 0)))
        if kind == "nt":
            m = w.shape[0]
            out_shapes.append(jax.ShapeDtypeStruct((n // seq, tps, m, tm), dt))
            out_specs.append(pl.BlockSpec((1, 1, m, tm), lambda i: (i // tps, i % tps, 0, 0)))
        else:
            m = w.shape[1]
            out_shapes.append(jax.ShapeDtypeStruct((n, m), dt))
            out_specs.append(pl.BlockSpec((tm, m), lambda i: (i, 0)))
    return pl.pallas_call(
        functools.partial(_proj_kernel, kinds=kinds),
        grid=(n // tm,),
        in_specs=in_specs,
        out_specs=out_specs,
        out_shape=out_shapes,
        compiler_params=_params(("arbitrary",)),
        name="proj",
    )(x2, *[w for w, _, _ in specs])


def _fox_cum_kernel(z_ref, b_ref, o_ref):
    seq = z_ref.shape[1]
    blk = CUM_BLOCK
    row = lax.broadcasted_iota(jnp.int32, (blk, blk), 0)
    col = lax.broadcasted_iota(jnp.int32, (blk, blk), 1)
    tri = (col <= row).astype(BF16)
    lane = lax.broadcasted_iota(jnp.int32, (blk, LANES), 1)
    carry = jnp.zeros((1, LANES), F32)
    for t in range(seq // blk):
        logf = jax.nn.log_sigmoid(z_ref[0, t * blk:(t + 1) * blk, :] + b_ref[...])
        hi, mid, lo = _split3(logf)
        c = (jnp.dot(tri, hi, preferred_element_type=F32)
             + jnp.dot(tri, mid, preferred_element_type=F32)
             + jnp.dot(tri, lo, preferred_element_type=F32)) + carry
        carry = c[blk - 1:blk, :]
        chi, cmid, clo = _split3(c)
        o_ref[0, t * blk:(t + 1) * blk, :] = jnp.where(
            lane < FOX_HEADS, chi, jnp.where(lane < 2 * FOX_HEADS, cmid, clo))


def _fox_cum(z3, bias3):
    b, s, _ = z3.shape
    return pl.pallas_call(
        _fox_cum_kernel,
        grid=(b,),
        in_specs=[pl.BlockSpec((1, s, LANES), lambda i: (i, 0, 0)),
                  pl.BlockSpec((1, LANES), lambda i: (0, 0))],
        out_specs=pl.BlockSpec((1, s, LANES), lambda i: (i, 0, 0)),
        out_shape=jax.ShapeDtypeStruct((b, s, LANES), BF16),
        compiler_params=_params(("arbitrary",)),
        name="fox_cum",
    )(z3, bias3)


def _softmax_step(s, vt, m, l, acc):
    m_new = jnp.maximum(m, jnp.max(s, axis=0, keepdims=True))
    alpha = jnp.exp(m - m_new)
    p = jnp.exp(s - m_new)
    l_new = alpha * l + jnp.sum(p, axis=0, keepdims=True)
    acc_new = alpha * acc + jnp.dot(vt, p.astype(BF16), preferred_element_type=F32)
    return m_new, l_new, acc_new


def _attn_init(tq):
    return tuple((jnp.full((1, tq), -jnp.inf, F32), jnp.zeros((1, tq), F32),
                  jnp.zeros((HEAD_DIM, tq), F32)) for _ in range(2))


def _attn_finish(carry, o_ref):
    o = jnp.concatenate([acc / l for _, l, acc in carry], axis=0)
    o_ref[0] = o.T.astype(o_ref.dtype)


def _fox_attn_kernel(q_ref, k_ref, cq_ref, ck_ref, vt_ref, o_ref, kaug_ref):
    tq = q_ref.shape[1]
    ntile = k_ref.shape[1] // tq
    j = pl.program_id(1)
    i = pl.program_id(2)
    lane = lax.broadcasted_iota(jnp.int32, (1, LANES), 1)
    prow = lax.broadcasted_iota(jnp.int32, (LANES, LANES), 0)
    pcol = lax.broadcasted_iota(jnp.int32, (LANES, LANES), 1)

    def place(h, first_lane, sign):
        hit = (((prow == h) & (pcol == first_lane))
               | ((prow == h + FOX_HEADS) & (pcol == first_lane + 1))
               | ((prow == h + 2 * FOX_HEADS) & (pcol == first_lane + 2)))
        return jnp.where(hit, sign, 0.0).astype(BF16)

    def augment(x, c3, h, hh, gate_off, sign, ones_off):
        base = HEAD_DIM * (1 - hh)
        head_lanes = (lane >= HEAD_DIM * hh) & (lane < HEAD_DIM * (hh + 1))
        ones = (lane >= base + ones_off) & (lane < base + ones_off + 3)
        aug = jnp.dot(c3, place(h, base + gate_off, sign), preferred_element_type=F32)
        return jnp.where(head_lanes, x, jnp.where(ones, 1.0, aug).astype(BF16))

    @pl.when(i == 0)
    def _():
        for hh in range(2):
            for t in range(ntile):
                kaug_ref[hh, t] = augment(k_ref[0, t * tq:(t + 1) * tq, :],
                                          ck_ref[0, t * tq:(t + 1) * tq, :],
                                          2 * j + hh, hh, 0, -1.0, 3)

    qaug = [augment(q_ref[0], cq_ref[0], 2 * j + hh, hh, 3, 1.0, 0) for hh in range(2)]
    causal = (lax.broadcasted_iota(jnp.int32, (tq, tq), 0)
              <= lax.broadcasted_iota(jnp.int32, (tq, tq), 1))

    def step(n, carry, diagonal):
        out = []
        for hh in range(2):
            s = lax.dot_general(kaug_ref[hh, n], qaug[hh], _NT, preferred_element_type=F32)
            if diagonal:
                s = jnp.where(causal, s, -jnp.inf)
            vt = vt_ref[0, n, HEAD_DIM * hh:HEAD_DIM * (hh + 1), :]
            out.append(_softmax_step(s, vt, *carry[hh]))
        return tuple(out)

    carry = step(i, _attn_init(tq), True)
    carry = lax.fori_loop(0, i, lambda n, c: step(n, c, False), carry)
    _attn_finish(carry, o_ref)


def _fox_attn(qk3, cum3, vt4):
    b, s, _ = qk3.shape
    t = ATT_TILE
    npair = FOX_HEADS // 2
    return pl.pallas_call(
        _fox_attn_kernel,
        grid=(b, npair, s // t),
        in_specs=[
            pl.BlockSpec((1, t, LANES), lambda b_, j, i: (b_, i, j)),
            pl.BlockSpec((1, s, LANES), lambda b_, j, i: (b_, 0, npair + j)),
            pl.BlockSpec((1, t, LANES), lambda b_, j, i: (b_, i, 0)),
            pl.BlockSpec((1, s, LANES), lambda b_, j, i: (b_, 0, 0)),
            pl.BlockSpec((1, s // t, LANES, t), lambda b_, j, i: (b_, 0, j, 0)),
        ],
        out_specs=pl.BlockSpec((1, t, LANES), lambda b_, j, i: (b_, i, j)),
        out_shape=jax.ShapeDtypeStruct((b, s, FOX_WIDTH), BF16),
        scratch_shapes=[pltpu.VMEM((2, s // t, t, LANES), BF16)],
        compiler_params=_params(("arbitrary", "arbitrary", "arbitrary")),
        name="fox_attn",
    )(qk3, qk3, cum3, cum3, vt4)


def _moba_attn_kernel(q_ref, k_ref, vt_ref, o_ref, km_ref, bias_ref):
    tq = q_ref.shape[1]
    seq = k_ref.shape[1]
    nb = seq // MOBA_BLOCK
    nbp = km_ref.shape[1] // 3
    bpt = tq // MOBA_BLOCK
    i = pl.program_id(2)
    lane = lax.broadcasted_iota(jnp.int32, (1, LANES), 1)

    @pl.when(i == 0)
    def _():
        rows = [jnp.sum(k_ref[0, n * MOBA_BLOCK:(n + 1) * MOBA_BLOCK, :].astype(F32),
                        axis=0, keepdims=True) * (1.0 / MOBA_BLOCK) for n in range(nb)]
        if nbp > nb:
            rows.append(jnp.zeros((nbp - nb, LANES), F32))
        kmean = jnp.concatenate(rows, axis=0)
        for hh in range(2):
            head_lanes = (lane >= HEAD_DIM * hh) & (lane < HEAD_DIM * (hh + 1))
            km_ref[hh] = jnp.concatenate(_split3(jnp.where(head_lanes, kmean, 0.0)), axis=0)

    q = q_ref[0]
    nidx = lax.broadcasted_iota(jnp.int32, (nbp, tq), 0)
    qblk = lax.shift_right_logical(
        i * tq + lax.broadcasted_iota(jnp.int32, (nbp, tq), 1), MOBA_BLOCK.bit_length() - 1)
    past = nidx < qblk
    qh = []
    for hh in range(2):
        head_lanes = (lane >= HEAD_DIM * hh) & (lane < HEAD_DIM * (hh + 1))
        qh.append(jnp.where(head_lanes, q, jnp.zeros_like(q)))
        g3 = lax.dot_general(km_ref[hh], qh[hh], _NT, preferred_element_type=F32)
        gate = (g3[0:nbp] + g3[nbp:2 * nbp] + g3[2 * nbp:3 * nbp]) * (1.0 / QK_SCALE)
        g = jnp.where(past, gate, -jnp.inf)
        sel = nidx == qblk
        for _ in range(MOBA_TOPK):
            first = jnp.min(jnp.where(g == jnp.max(g, axis=0, keepdims=True), nidx, nbp),
                            axis=0, keepdims=True)
            pick = nidx == first
            sel = sel | (pick & past)
            g = jnp.where(pick, -jnp.inf, g)
        bias = jnp.where(sel, 0.0, NEG_BIG)
        for n in range(nb):
            bias_ref[hh, n] = bias[n:n + 1, :]

    causal = (lax.broadcasted_iota(jnp.int32, (tq, tq), 0)
              <= lax.broadcasted_iota(jnp.int32, (tq, tq), 1))

    def step(n, carry, diagonal):
        kt = k_ref[0, pl.ds(pl.multiple_of(n * tq, tq), tq), :]
        out = []
        for hh in range(2):
            s = lax.dot_general(kt, qh[hh], _NT, preferred_element_type=F32)
            s = s + jnp.concatenate(
                [jnp.broadcast_to(bias_ref[hh, n * bpt + c], (MOBA_BLOCK, tq)) for c in range(bpt)],
                axis=0)
            if diagonal:
                s = jnp.where(causal, s, -jnp.inf)
            vt = vt_ref[0, n, HEAD_DIM * hh:HEAD_DIM * (hh + 1), :]
            out.append(_softmax_step(s, vt, *carry[hh]))
        return tuple(out)

    carry = step(i, _attn_init(tq), True)
    carry = lax.fori_loop(0, i, lambda n, c: step(n, c, False), carry)
    _attn_finish(carry, o_ref)


def _moba_attn(qk3, vt4):
    b, s, _ = qk3.shape
    t = ATT_TILE
    npair = MOBA_HEADS // 2
    nb = s // MOBA_BLOCK
    nbp = -(-nb // 16) * 16
    return pl.pallas_call(
        _moba_attn_kernel,
        grid=(b, npair, s // t),
        in_specs=[
            pl.BlockSpec((1, t, LANES), lambda b_, j, i: (b_, i, j)),
            pl.BlockSpec((1, s, LANES), lambda b_, j, i: (b_, 0, npair + j)),
            pl.BlockSpec((1, s // t, LANES, t), lambda b_, j, i: (b_, 0, j, 0)),
        ],
        out_specs=pl.BlockSpec((1, t, LANES), lambda b_, j, i: (b_, i, j)),
        out_shape=jax.ShapeDtypeStruct((b, s, MOBA_WIDTH), BF16),
        scratch_shapes=[pltpu.VMEM((2, 3 * nbp, LANES), BF16),
                        pltpu.VMEM((2, nb, 1, t), F32)],
        compiler_params=_params(("arbitrary", "arbitrary", "arbitrary")),
        name="moba_attn",
    )(qk3, qk3, vt4)


def _pool_kernel(u_ref, w_ref, sc_ref, o_ref):
    g = pl.program_id(1)
    u = u_ref[0]
    seq = u.shape[0]
    t = lax.broadcasted_iota(jnp.int32, u.shape, 0)
    wsum = u
    acc = jnp.zeros_like(u)
    for r, w in enumerate(POOL_WINDOWS):
        d = w // 2
        wsum = wsum + jnp.where(t >= d, pltpu.roll(wsum, d, 0), 0.0)
        cnt = jnp.minimum(t + 1, w).astype(F32)
        acc = jnp.where(g == r, wsum / cnt, acc)
    mixed = (acc - u).astype(BF16)
    y = jnp.dot(mixed, w_ref[0], preferred_element_type=F32) * sc_ref[...]
    o_ref[0] = y.astype(o_ref.dtype)
    del seq


def _pool(u3, pool_w, pool_scale):
    b, s, _ = u3.shape
    ng = len(POOL_WINDOWS)
    return pl.pallas_call(
        _pool_kernel,
        grid=(b, ng),
        in_specs=[pl.BlockSpec((1, s, LANES), lambda b_, g: (b_, 0, g)),
                  pl.BlockSpec((1, POOL_GROUP_DIM, POOL_GROUP_DIM), lambda b_, g: (g, 0, 0)),
                  pl.BlockSpec((1, LANES), lambda b_, g: (0, g))],
        out_specs=pl.BlockSpec((1, s, LANES), lambda b_, g: (b_, 0, g)),
        out_shape=jax.ShapeDtypeStruct((b, s, POOL_WIDTH), BF16),
        compiler_params=_params(("arbitrary", "arbitrary")),
        name="pool",
    )(u3, pool_w, pool_scale)


def _outproj_ln_kernel(*refs, n_in):
    a_refs = refs[:n_in]
    w_refs = refs[n_in:2 * n_in]
    x_ref, g_ref, b_ref, o_ref = refs[2 * n_in:]
    mix = jnp.dot(a_refs[0][...], w_refs[0][...], preferred_element_type=F32)
    for a_ref, w_ref in zip(a_refs[1:], w_refs[1:]):
        mix = mix + jnp.dot(a_ref[...], w_ref[...], preferred_element_type=F32)
    o_ref[...] = _ln(ALPHA * x_ref[...] + mix, g_ref[...], b_ref[...])


def _outproj_ln(acts, weights, x2, gamma, beta):
    n, d = x2.shape
    tm = ROW_TILE
    in_specs = [pl.BlockSpec((tm, a.shape[1]), lambda i: (i, 0)) for a in acts]
    in_specs += [pl.BlockSpec(w.shape, lambda i: (0, 0)) for w in weights]
    in_specs += [pl.BlockSpec((tm, d), lambda i: (i, 0)),
                 pl.BlockSpec((1, d), lambda i: (0, 0)),
                 pl.BlockSpec((1, d), lambda i: (0, 0))]
    return pl.pallas_call(
        functools.partial(_outproj_ln_kernel, n_in=len(acts)),
        grid=(n // tm,),
        in_specs=in_specs,
        out_specs=pl.BlockSpec((tm, d), lambda i: (i, 0)),
        out_shape=jax.ShapeDtypeStruct((n, d), F32),
        compiler_params=_params(("arbitrary",)),
        name="outproj_ln",
    )(*acts, *weights, x2, gamma, beta)


def _ffn_kernel(*refs, n_exp):
    if n_exp:
        x_ref, c_ref, wg_ref, wu_ref, wd_ref, g_ref, b_ref, o_ref, acc_ref, xb_ref = refs
        e, f = pl.program_id(1), pl.program_id(2)
        first = (e == 0) & (f == 0)
        last = (e == n_exp - 1) & (f == pl.num_programs(2) - 1)
    else:
        x_ref, wg_ref, wu_ref, wd_ref, g_ref, b_ref, o_ref, acc_ref, xb_ref = refs
        f = pl.program_id(1)
        first = f == 0
        last = f == pl.num_programs(1) - 1

    @pl.when(first)
    def _():
        acc_ref[...] = jnp.zeros_like(acc_ref)
        xb_ref[...] = x_ref[...].astype(BF16)

    xb = xb_ref[...]
    hid = (jax.nn.silu(jnp.dot(xb, wg_ref[...], preferred_element_type=F32))
           * jnp.dot(xb, wu_ref[...], preferred_element_type=F32))
    y = jnp.dot(hid.astype(BF16), wd_ref[...], preferred_element_type=F32)
    if n_exp:
        lane = lax.broadcasted_iota(jnp.int32, c_ref.shape, 1)
        y = y * jnp.sum(jnp.where(lane == e, c_ref[...], 0.0), axis=1, keepdims=True)
    acc_ref[...] += y

    @pl.when(last)
    def _():
        o_ref[...] = _ln(ALPHA * x_ref[...] + acc_ref[...], g_ref[...], b_ref[...])


def _ffn_ln(x2, wg, wu, wd, gamma, beta, combine=None):
    n, d = x2.shape
    tm, tf = FFN_ROW_TILE, FFN_COL_TILE
    if combine is None:
        ff = wg.shape[1]
        grid = (n // tm, ff // tf)
        row = lambda i, f: (i, 0)
        fix = lambda i, f: (0, 0)
        w_specs = [pl.BlockSpec((d, tf), lambda i, f: (0, f)),
                   pl.BlockSpec((d, tf), lambda i, f: (0, f)),
                   pl.BlockSpec((tf, d), lambda i, f: (f, 0))]
        extra_in, extra_specs = [], []
        n_exp = 0
    else:
        n_exp, _, ff = wg.shape
        grid = (n // tm, n_exp, ff // tf)
        row = lambda i, e, f: (i, 0)
        fix = lambda i, e, f: (0, 0)
        w_specs = [pl.BlockSpec((None, d, tf), lambda i, e, f: (e, 0, f)),
                   pl.BlockSpec((None, d, tf), lambda i, e, f: (e, 0, f)),
                   pl.BlockSpec((None, tf, d), lambda i, e, f: (e, f, 0))]
        extra_in, extra_specs = [combine], [pl.BlockSpec((tm, LANES), row)]
    return pl.pallas_call(
        functools.partial(_ffn_kernel, n_exp=n_exp),
        grid=grid,
        in_specs=[pl.BlockSpec((tm, d), row)] + extra_specs + w_specs
                 + [pl.BlockSpec((1, d), fix), pl.BlockSpec((1, d), fix)],
        out_specs=pl.BlockSpec((tm, d), row),
        out_shape=jax.ShapeDtypeStruct((n, d), F32),
        scratch_shapes=[pltpu.VMEM((tm, d), F32), pltpu.VMEM((tm, d), BF16)],
        compiler_params=_params(("arbitrary",) * len(grid)),
        name="moe_ffn_ln" if n_exp else "ffn_ln",
    )(x2, *extra_in, wg, wu, wd, gamma, beta)


def _router_kernel(x_ref, w_ref, o_ref):
    x = x_ref[...]
    xh = x.astype(BF16)
    xm = (x - xh.astype(F32)).astype(BF16)
    l3 = (jnp.dot(xh, w_ref[...], preferred_element_type=F32)
          + jnp.dot(xm, w_ref[...], preferred_element_type=F32))
    logits = l3[:, 0:LANES] + l3[:, LANES:2 * LANES] + l3[:, 2 * LANES:3 * LANES]
    lane = lax.broadcasted_iota(jnp.int32, logits.shape, 1)
    g = jnp.where(lane < N_EXPERTS, logits, -jnp.inf)
    v1 = jnp.max(g, axis=1, keepdims=True)
    p1 = lane == jnp.min(jnp.where(g == v1, lane, LANES), axis=1, keepdims=True)
    g = jnp.where(p1, -jnp.inf, g)
    v2 = jnp.max(g, axis=1, keepdims=True)
    p2 = lane == jnp.min(jnp.where(g == v2, lane, LANES), axis=1, keepdims=True)
    e2 = jnp.exp(v2 - v1)
    den = 1.0 + e2
    o_ref[...] = jnp.where(p1, 1.0 / den, 0.0) + jnp.where(p2, e2 / den, 0.0)


def _router(x2, w3):
    n, d = x2.shape
    tm = ROW_TILE
    return pl.pallas_call(
        _router_kernel,
        grid=(n // tm,),
        in_specs=[pl.BlockSpec((tm, d), lambda i: (i, 0)),
                  pl.BlockSpec(w3.shape, lambda i: (0, 0))],
        out_specs=pl.BlockSpec((tm, LANES), lambda i: (i, 0)),
        out_shape=jax.ShapeDtypeStruct((n, LANES), F32),
        compiler_params=_params(("arbitrary",)),
        name="router",
    )(x2, w3)


def _pad_cols(w, width):
    return jnp.pad(w, ((0, 0), (0, width - w.shape[1])))


def _even_weights(w_in, b_forget):
    wq, wk, wv, wf, wu = jnp.split(
        w_in, [FOX_WIDTH, 2 * FOX_WIDTH, 3 * FOX_WIDTH, 3 * FOX_WIDTH + FOX_HEADS], axis=1)
    w_qk = jnp.concatenate([wq * QK_SCALE, wk], axis=1).astype(BF16)
    w_vt = wv.T.astype(BF16)
    w_f3 = _pad_cols(jnp.concatenate([wf, wf, wf], axis=1), LANES).astype(BF16)
    b_f3 = _pad_cols(jnp.concatenate([b_forget, b_forget, b_forget])[None, :], LANES)
    return w_qk, w_vt, w_f3, b_f3, wu.astype(BF16)


def _odd_weights(w_in):
    wq, wk, wv = jnp.split(w_in, 3, axis=1)
    return jnp.concatenate([wq * QK_SCALE, wk], axis=1).astype(BF16), wv.T.astype(BF16)


def _router_weights(w_router):
    return jnp.concatenate([_pad_cols(p.astype(F32), LANES).astype(BF16)
                            for p in _split3(w_router)], axis=1)


def kernel(x, ln1_g, ln1_b, ln2_g, ln2_b, ev_w_in, ev_b_forget, ev_pool_w, ev_pool_scale, ev_w_out,
           ev_ffn_gate, ev_ffn_up, ev_ffn_down, od_w_in, od_w_out, od_router, od_exp_gate, od_exp_up,
           od_exp_down):
    bsz, seq, d = x.shape
    n = bsz * seq
    x2 = x.reshape(n, d)
    for layer in range(DEPTH):
        li = layer // 2
        g1, b1 = ln1_g[layer][None, :], ln1_b[layer][None, :]
        g2, b2 = ln2_g[layer][None, :], ln2_b[layer][None, :]
        if layer % 2 == 0:
            w_qk, w_vt, w_f3, b_f3, w_u = _even_weights(ev_w_in[li], ev_b_forget[li])
            qk, vt, z, u = _proj(x2, seq, [(w_qk, "nn", BF16), (w_vt, "nt", BF16),
                                           (w_f3, "nn", F32), (w_u, "nn", F32)])
            cum3 = _fox_cum(z.reshape(bsz, seq, LANES), b_f3)
            att = _fox_attn(qk.reshape(bsz, seq, -1), cum3, vt)
            pooled = _pool(u.reshape(bsz, seq, POOL_WIDTH), ev_pool_w[li].astype(BF16),
                           ev_pool_scale[li][None, :])
            w_out = ev_w_out[li].astype(BF16)
            x2 = _outproj_ln([att.reshape(n, FOX_WIDTH), pooled.reshape(n, POOL_WIDTH)],
                             [w_out[:FOX_WIDTH], w_out[FOX_WIDTH:]], x2, g1, b1)
            x2 = _ffn_ln(x2, ev_ffn_gate[li].astype(BF16), ev_ffn_up[li].astype(BF16),
                         ev_ffn_down[li].astype(BF16), g2, b2)
        else:
            w_qk, w_vt = _odd_weights(od_w_in[li])
            qk, vt = _proj(x2, seq, [(w_qk, "nn", BF16), (w_vt, "nt", BF16)])
            att = _moba_attn(qk.reshape(bsz, seq, -1), vt)
            x2 = _outproj_ln([att.reshape(n, MOBA_WIDTH)], [od_w_out[li].astype(BF16)], x2, g1, b1)
            combine = _router(x2, _router_weights(od_router[li]))
            x2 = _ffn_ln(x2, od_exp_gate[li].astype(BF16), od_exp_up[li].astype(BF16),
                         od_exp_down[li].astype(BF16), g2, b2, combine=combine)
    return x2.reshape(bsz, seq, d)
```

```python
import functools

import jax
import jax.numpy as jnp
from jax import lax
from jax.experimental import pallas as pl
from jax.experimental.pallas import tpu as pltpu

F32 = jnp.float32
BF16 = jnp.bfloat16

D_MODEL = 1024
DEPTH = 4
HEAD_DIM = 64
FOX_HEADS = 8
FOX_WIDTH = FOX_HEADS * HEAD_DIM
POOL_WINDOWS = (2, 4, 8, 16)
POOL_WIDTH = 512
POOL_GROUP_DIM = 128
MOBA_HEADS = 16
MOBA_WIDTH = MOBA_HEADS * HEAD_DIM
MOBA_BLOCK = 256
MOBA_TOPK = 3
D_FF = 3584
N_EXPERTS = 8
LN_EPS = 1e-5
ALPHA = (2 * DEPTH) ** 0.25
QK_SCALE = HEAD_DIM ** -0.5

LANES = 128
ATT_TILE = 512
ROW_TILE = 512
FFN_ROW_TILE = 1024
FFN_COL_TILE = 512
MOE_ROW_TILE = 512
CUM_BLOCK = 256
NEG_BIG = -1e30
VMEM_LIMIT = 48 * 1024 * 1024

_NT = (((1,), (1,)), ((), ()))


def _params(sem):
    return pltpu.CompilerParams(dimension_semantics=sem, vmem_limit_bytes=VMEM_LIMIT)


def _split3(x):
    hi = x.astype(BF16)
    r = x - hi.astype(F32)
    mid = r.astype(BF16)
    lo = (r - mid.astype(F32)).astype(BF16)
    return hi, mid, lo


def _ln(y, g, b):
    mu = jnp.mean(y, axis=-1, keepdims=True)
    yc = y - mu
    var = jnp.mean(yc * yc, axis=-1, keepdims=True)
    return yc * lax.rsqrt(var + LN_EPS) * g + b


def _proj_kernel(*refs, kinds):
    x_ref = refs[0]
    w_refs = refs[1:1 + len(kinds)]
    o_refs = refs[1 + len(kinds):]
    xb = x_ref[...].astype(BF16)
    for kind, w_ref, o_ref in zip(kinds, w_refs, o_refs):
        if kind == "nt":
            m = w_ref.shape[0]
            for c0 in range(0, m, 512):
                c1 = min(c0 + 512, m)
                o_ref[0, 0, c0:c1, :] = lax.dot_general(
                    w_ref[c0:c1, :], xb, _NT, preferred_element_type=F32).astype(o_ref.dtype)
        else:
            m = w_ref.shape[1]
            for c0 in range(0, m, 512):
                c1 = min(c0 + 512, m)
                o_ref[:, c0:c1] = jnp.dot(
                    xb, w_ref[:, c0:c1], preferred_element_type=F32).astype(o_ref.dtype)


def _proj(x2, seq, specs):
    n, d = x2.shape
    tm = ROW_TILE
    tps = seq // tm
    kinds = tuple(k for _, k, _ in specs)
    in_specs = [pl.BlockSpec((tm, d), lambda i: (i, 0))]
    out_specs, out_shapes = [], []
    for w, kind, dt in specs:
        in_specs.append(pl.BlockSpec(w.shape, lambda i: (0, 0)))
        if kind == "nt":
            m = w.shape[0]
            out_shapes.append(jax.ShapeDtypeStruct((n // seq, tps, m, tm), dt))
            out_specs.append(pl.BlockSpec((1, 1, m, tm), lambda i: (i // tps, i % tps, 0, 0)))
        else:
            m = w.shape[1]
            out_shapes.append(jax.ShapeDtypeStruct((n, m), dt))
            out_specs.append(pl.BlockSpec((tm, m), lambda i: (i, 0)))
    return pl.pallas_call(
        functools.partial(_proj_kernel, kinds=kinds),
        grid=(n // tm,),
        in_specs=in_specs,
        out_specs=out_specs,
        out_shape=out_shapes,
        compiler_params=_params(("arbitrary",)),
        name="proj",
    )(x2, *[w for w, _, _ in specs])


def _fox_cum_kernel(z_ref, b_ref, o_ref):
    seq = z_ref.shape[1]
    blk = CUM_BLOCK
    row = lax.broadcasted_iota(jnp.int32, (blk, blk), 0)
    col = lax.broadcasted_iota(jnp.int32, (blk, blk), 1)
    tri = (col <= row).astype(BF16)
    lane = lax.broadcasted_iota(jnp.int32, (blk, LANES), 1)
    carry = jnp.zeros((1, LANES), F32)
    for t in range(seq // blk):
        logf = jax.nn.log_sigmoid(z_ref[0, t * blk:(t + 1) * blk, :] + b_ref[...])
        hi, mid, lo = _split3(logf)
        c = (jnp.dot(tri, hi, preferred_element_type=F32)
             + jnp.dot(tri, mid, preferred_element_type=F32)
             + jnp.dot(tri, lo, preferred_element_type=F32)) + carry
        carry = c[blk - 1:blk, :]
        chi, cmid, clo = _split3(c)
        o_ref[0, t * blk:(t + 1) * blk, :] = jnp.where(
            lane < FOX_HEADS, chi, jnp.where(lane < 2 * FOX_HEADS, cmid, clo))


def _fox_cum(z3, bias3):
    b, s, _ = z3.shape
    return pl.pallas_call(
        _fox_cum_kernel,
        grid=(b,),
        in_specs=[pl.BlockSpec((1, s, LANES), lambda i: (i, 0, 0)),
                  pl.BlockSpec((1, LANES), lambda i: (0, 0))],
        out_specs=pl.BlockSpec((1, s, LANES), lambda i: (i, 0, 0)),
        out_shape=jax.ShapeDtypeStruct((b, s, LANES), BF16),
        compiler_params=_params(("arbitrary",)),
        name="fox_cum",
    )(z3, bias3)


def _softmax_step(s, vt, m, l, acc):
    m_new = jnp.maximum(m, jnp.max(s, axis=0, keepdims=True))
    alpha = jnp.exp(m - m_new)
    p = jnp.exp(s - m_new)
    l_new = alpha * l + jnp.sum(p, axis=0, keepdims=True)
    acc_new = alpha * acc + jnp.dot(vt, p.astype(BF16), preferred_element_type=F32)
    return m_new, l_new, acc_new


def _attn_init(tq):
    return tuple((jnp.full((1, tq), -jnp.inf, F32), jnp.zeros((1, tq), F32),
                  jnp.zeros((HEAD_DIM, tq), F32)) for _ in range(2))


def _attn_finish(carry, o_ref):
    o = jnp.concatenate([acc / l for _, l, acc in carry], axis=0)
    o_ref[0] = o.T.astype(o_ref.dtype)


def _fox_attn_kernel(q_ref, k_ref, cq_ref, ck_ref, vt_ref, o_ref, kaug_ref):
    tq = q_ref.shape[1]
    ntile = k_ref.shape[1] // tq
    j = pl.program_id(1)
    i = pl.program_id(2)
    lane = lax.broadcasted_iota(jnp.int32, (1, LANES), 1)
    prow = lax.broadcasted_iota(jnp.int32, (LANES, LANES), 0)
    pcol = lax.broadcasted_iota(jnp.int32, (LANES, LANES), 1)

    def place(h, first_lane, sign):
        hit = (((prow == h) & (pcol == first_lane))
               | ((prow == h + FOX_HEADS) & (pcol == first_lane + 1))
               | ((prow == h + 2 * FOX_HEADS) & (pcol == first_lane + 2)))
        return jnp.where(hit, sign, 0.0).astype(BF16)

    def augment(x, c3, h, hh, gate_off, sign, ones_off):
        base = HEAD_DIM * (1 - hh)
        head_lanes = (lane >= HEAD_DIM * hh) & (lane < HEAD_DIM * (hh + 1))
        ones = (lane >= base + ones_off) & (lane < base + ones_off + 3)
        aug = jnp.dot(c3, place(h, base + gate_off, sign), preferred_element_type=F32)
        return jnp.where(head_lanes, x, jnp.where(ones, 1.0, aug).astype(BF16))

    @pl.when(i == 0)
    def _():
        for hh in range(2):
            for t in range(ntile):
                kaug_ref[hh, t] = augment(k_ref[0, t * tq:(t + 1) * tq, :],
                                          ck_ref[0, t * tq:(t + 1) * tq, :],
                                          2 * j + hh, hh, 0, -1.0, 3)

    qaug = [augment(q_ref[0], cq_ref[0], 2 * j + hh, hh, 3, 1.0, 0) for hh in range(2)]
    causal = (lax.broadcasted_iota(jnp.int32, (tq, tq), 0)
              <= lax.broadcasted_iota(jnp.int32, (tq, tq), 1))

    def step(n, carry, diagonal):
        out = []
        for hh in range(2):
            s = lax.dot_general(kaug_ref[hh, n], qaug[hh], _NT, preferred_element_type=F32)
            if diagonal:
                s = jnp.where(causal, s, -jnp.inf)
            vt = vt_ref[0, n, HEAD_DIM * hh:HEAD_DIM * (hh + 1), :]
            out.append(_softmax_step(s, vt, *carry[hh]))
        return tuple(out)

    carry = step(i, _attn_init(tq), True)
    carry = lax.fori_loop(0, i, lambda n, c: step(n, c, False), carry)
    _attn_finish(carry, o_ref)


def _fox_attn(qk3, cum3, vt4):
    b, s, _ = qk3.shape
    t = ATT_TILE
    npair = FOX_HEADS // 2
    return pl.pallas_call(
        _fox_attn_kernel,
        grid=(b, npair, s // t),
        in_specs=[
            pl.BlockSpec((1, t, LANES), lambda b_, j, i: (b_, i, j)),
            pl.BlockSpec((1, s, LANES), lambda b_, j, i: (b_, 0, npair + j)),
            pl.BlockSpec((1, t, LANES), lambda b_, j, i: (b_, i, 0)),
            pl.BlockSpec((1, s, LANES), lambda b_, j, i: (b_, 0, 0)),
            pl.BlockSpec((1, s // t, LANES, t), lambda b_, j, i: (b_, 0, j, 0)),
        ],
        out_specs=pl.BlockSpec((1, t, LANES), lambda b_, j, i: (b_, i, j)),
        out_shape=jax.ShapeDtypeStruct((b, s, FOX_WIDTH), BF16),
        scratch_shapes=[pltpu.VMEM((2, s // t, t, LANES), BF16)],
        compiler_params=_params(("arbitrary", "arbitrary", "arbitrary")),
        name="fox_attn",
    )(qk3, qk3, cum3, cum3, vt4)


def _moba_attn_kernel(q_ref, k_ref, vt_ref, o_ref, km_ref, bias_ref):
    tq = q_ref.shape[1]
    seq = k_ref.shape[1]
    nb = seq // MOBA_BLOCK
    nbp = km_ref.shape[1] // 3
    bpt = tq // MOBA_BLOCK
    i = pl.program_id(2)
    lane = lax.broadcasted_iota(jnp.int32, (1, LANES), 1)

    @pl.when(i == 0)
    def _():
        rows = [jnp.sum(k_ref[0, n * MOBA_BLOCK:(n + 1) * MOBA_BLOCK, :].astype(F32),
                        axis=0, keepdims=True) * (1.0 / MOBA_BLOCK) for n in range(nb)]
        if nbp > nb:
            rows.append(jnp.zeros((nbp - nb, LANES), F32))
        kmean = jnp.concatenate(rows, axis=0)
        for hh in range(2):
            head_lanes = (lane >= HEAD_DIM * hh) & (lane < HEAD_DIM * (hh + 1))
            km_ref[hh] = jnp.concatenate(_split3(jnp.where(head_lanes, kmean, 0.0)), axis=0)

    q = q_ref[0]
    nidx = lax.broadcasted_iota(jnp.int32, (nbp, tq), 0)
    qblk = lax.shift_right_logical(
        i * tq + lax.broadcasted_iota(jnp.int32, (nbp, tq), 1), MOBA_BLOCK.bit_length() - 1)
    past = nidx < qblk
    qh = []
    for hh in range(2):
        head_lanes = (lane >= HEAD_DIM * hh) & (lane < HEAD_DIM * (hh + 1))
        qh.append(jnp.where(head_lanes, q, jnp.zeros_like(q)))
        g3 = lax.dot_general(km_ref[hh], qh[hh], _NT, preferred_element_type=F32)
        gate = (g3[0:nbp] + g3[nbp:2 * nbp] + g3[2 * nbp:3 * nbp]) * (1.0 / QK_SCALE)
        g = jnp.where(past, gate, -jnp.inf)
        sel = nidx == qblk
        for _ in range(MOBA_TOPK):
            first = jnp.min(jnp.where(g == jnp.max(g, axis=0, keepdims=True), nidx, nbp),
                            axis=0, keepdims=True)
            pick = nidx == first
            sel = sel | (pick & past)
            g = jnp.where(pick, -jnp.inf, g)
        bias = jnp.where(sel, 0.0, NEG_BIG)
        for n in range(nb):
            bias_ref[hh, n] = bias[n:n + 1, :]

    causal = (lax.broadcasted_iota(jnp.int32, (tq, tq), 0)
              <= lax.broadcasted_iota(jnp.int32, (tq, tq), 1))

    def step(n, carry, diagonal):
        kt = k_ref[0, pl.ds(pl.multiple_of(n * tq, tq), tq), :]
        out = []
        for hh in range(2):
            s = lax.dot_general(kt, qh[hh], _NT, preferred_element_type=F32)
            s = s + jnp.concatenate(
                [jnp.broadcast_to(bias_ref[hh, n * bpt + c], (MOBA_BLOCK, tq)) for c in range(bpt)],
                axis=0)
            if diagonal:
                s = jnp.where(causal, s, -jnp.inf)
            vt = vt_ref[0, n, HEAD_DIM * hh:HEAD_DIM * (hh + 1), :]
            out.append(_softmax_step(s, vt, *carry[hh]))
        return tuple(out)

    carry = step(i, _attn_init(tq), True)
    carry = lax.fori_loop(0, i, lambda n, c: step(n, c, False), carry)
    _attn_finish(carry, o_ref)


def _moba_attn(qk3, vt4):
    b, s, _ = qk3.shape
    t = ATT_TILE
    npair = MOBA_HEADS // 2
    nb = s // MOBA_BLOCK
    nbp = -(-nb // 16) * 16
    return pl.pallas_call(
        _moba_attn_kernel,
        grid=(b, npair, s // t),
        in_specs=[
            pl.BlockSpec((1, t, LANES), lambda b_, j, i: (b_, i, j)),
            pl.BlockSpec((1, s, LANES), lambda b_, j, i: (b_, 0, npair + j)),
            pl.BlockSpec((1, s // t, LANES, t), lambda b_, j, i: (b_, 0, j, 0)),
        ],
        out_specs=pl.BlockSpec((1, t, LANES), lambda b_, j, i: (b_, i, j)),
        out_shape=jax.ShapeDtypeStruct((b, s, MOBA_WIDTH), BF16),
        scratch_shapes=[pltpu.VMEM((2, 3 * nbp, LANES), BF16),
                        pltpu.VMEM((2, nb, 1, t), F32)],
        compiler_params=_params(("arbitrary", "arbitrary", "arbitrary")),
        name="moba_attn",
    )(qk3, qk3, vt4)


def _pool_kernel(u_ref, w_ref, sc_ref, o_ref):
    g = pl.program_id(1)
    u = u_ref[0]
    seq = u.shape[0]
    t = lax.broadcasted_iota(jnp.int32, u.shape, 0)
    wsum = u
    acc = jnp.zeros_like(u)
    for r, w in enumerate(POOL_WINDOWS):
        d = w // 2
        wsum = wsum + jnp.where(t >= d, pltpu.roll(wsum, d, 0), 0.0)
        cnt = jnp.minimum(t + 1, w).astype(F32)
        acc = jnp.where(g == r, wsum / cnt, acc)
    mixed = (acc - u).astype(BF16)
    y = jnp.dot(mixed, w_ref[0], preferred_element_type=F32) * sc_ref[...]
    o_ref[0] = y.astype(o_ref.dtype)
    del seq


def _pool(u3, pool_w, pool_scale):
    b, s, _ = u3.shape
    ng = len(POOL_WINDOWS)
    return pl.pallas_call(
        _pool_kernel,
        grid=(b, ng),
        in_specs=[pl.BlockSpec((1, s, LANES), lambda b_, g: (b_, 0, g)),
                  pl.BlockSpec((1, POOL_GROUP_DIM, POOL_GROUP_DIM), lambda b_, g: (g, 0, 0)),
                  pl.BlockSpec((1, LANES), lambda b_, g: (0, g))],
        out_specs=pl.BlockSpec((1, s, LANES), lambda b_, g: (b_, 0, g)),
        out_shape=jax.ShapeDtypeStruct((b, s, POOL_WIDTH), BF16),
        compiler_params=_params(("arbitrary", "arbitrary")),
        name="pool",
    )(u3, pool_w, pool_scale)


def _outproj_ln_kernel(*refs, n_in):
    a_refs = refs[:n_in]
    w_refs = refs[n_in:2 * n_in]
    x_ref, g_ref, b_ref, o_ref = refs[2 * n_in:]
    mix = jnp.dot(a_refs[0][...], w_refs[0][...], preferred_element_type=F32)
    for a_ref, w_ref in zip(a_refs[1:], w_refs[1:]):
        mix = mix + jnp.dot(a_ref[...], w_ref[...], preferred_element_type=F32)
    o_ref[...] = _ln(ALPHA * x_ref[...] + mix, g_ref[...], b_ref[...])


def _outproj_ln(acts, weights, x2, gamma, beta):
    n, d = x2.shape
    tm = ROW_TILE
    in_specs = [pl.BlockSpec((tm, a.shape[1]), lambda i: (i, 0)) for a in acts]
    in_specs += [pl.BlockSpec(w.shape, lambda i: (0, 0)) for w in weights]
    in_specs += [pl.BlockSpec((tm, d), lambda i: (i, 0)),
                 pl.BlockSpec((1, d), lambda i: (0, 0)),
                 pl.BlockSpec((1, d), lambda i: (0, 0))]
    return pl.pallas_call(
        functools.partial(_outproj_ln_kernel, n_in=len(acts)),
        grid=(n // tm,),
        in_specs=in_specs,
        out_specs=pl.BlockSpec((tm, d), lambda i: (i, 0)),
        out_shape=jax.ShapeDtypeStruct((n, d), F32),
        compiler_params=_params(("arbitrary",)),
        name="outproj_ln",
    )(*acts, *weights, x2, gamma, beta)


def _swiglu_chunk(xb, wg_ref, wu_ref, wd_ref):
    hid = (jax.nn.silu(jnp.dot(xb, wg_ref[...], preferred_element_type=F32))
           * jnp.dot(xb, wu_ref[...], preferred_element_type=F32))
    return jnp.dot(hid.astype(BF16), wd_ref[...], preferred_element_type=F32)


def _ffn_kernel(x_ref, wg_ref, wu_ref, wd_ref, g_ref, b_ref, o_ref, acc_ref, xb_ref):
    f = pl.program_id(1)

    @pl.when(f == 0)
    def _():
        acc_ref[...] = jnp.zeros_like(acc_ref)
        xb_ref[...] = x_ref[...].astype(BF16)

    acc_ref[...] += _swiglu_chunk(xb_ref[...], wg_ref, wu_ref, wd_ref)

    @pl.when(f == pl.num_programs(1) - 1)
    def _():
        o_ref[...] = _ln(ALPHA * x_ref[...] + acc_ref[...], g_ref[...], b_ref[...])


def _ffn_ln(x2, wg, wu, wd, gamma, beta):
    n, d = x2.shape
    tm, tf = FFN_ROW_TILE, FFN_COL_TILE
    ff = wg.shape[1]
    return pl.pallas_call(
        _ffn_kernel,
        grid=(n // tm, ff // tf),
        in_specs=[pl.BlockSpec((tm, d), lambda i, f: (i, 0)),
                  pl.BlockSpec((d, tf), lambda i, f: (0, f)),
                  pl.BlockSpec((d, tf), lambda i, f: (0, f)),
                  pl.BlockSpec((tf, d), lambda i, f: (f, 0)),
                  pl.BlockSpec((1, d), lambda i, f: (0, 0)),
                  pl.BlockSpec((1, d), lambda i, f: (0, 0))],
        out_specs=pl.BlockSpec((tm, d), lambda i, f: (i, 0)),
        out_shape=jax.ShapeDtypeStruct((n, d), F32),
        scratch_shapes=[pltpu.VMEM((tm, d), F32), pltpu.VMEM((tm, d), BF16)],
        compiler_params=_params(("arbitrary", "arbitrary")),
        name="ffn_ln",
    )(x2, wg, wu, wd, gamma, beta)


def _router_kernel(x_ref, w_ref, info_ref, cnt_ref, seen_ref):
    @pl.when(pl.program_id(0) == 0)
    def _():
        seen_ref[...] = jnp.zeros_like(seen_ref)

    x = x_ref[...]
    tm = x.shape[0]
    xh = x.astype(BF16)
    xm = (x - xh.astype(F32)).astype(BF16)
    l3 = (jnp.dot(xh, w_ref[...], preferred_element_type=F32)
          + jnp.dot(xm, w_ref[...], preferred_element_type=F32))
    logits = l3[:, 0:LANES] + l3[:, LANES:2 * LANES] + l3[:, 2 * LANES:3 * LANES]
    lane = lax.broadcasted_iota(jnp.int32, logits.shape, 1)
    g = jnp.where(lane < N_EXPERTS, logits, -jnp.inf)
    v1 = jnp.max(g, axis=1, keepdims=True)
    i1 = jnp.min(jnp.where(g == v1, lane, LANES), axis=1, keepdims=True)
    p1 = lane == i1
    g = jnp.where(p1, -jnp.inf, g)
    v2 = jnp.max(g, axis=1, keepdims=True)
    i2 = jnp.min(jnp.where(g == v2, lane, LANES), axis=1, keepdims=True)
    p2 = lane == i2
    e2 = jnp.exp(v2 - v1)
    den = 1.0 + e2
    member = jnp.where(p1 | p2, 1.0, 0.0)
    earlier = (lax.broadcasted_iota(jnp.int32, (tm, tm), 1)
               < lax.broadcasted_iota(jnp.int32, (tm, tm), 0)).astype(BF16)
    before = jnp.dot(earlier, member.astype(BF16), preferred_element_type=F32) + seen_ref[...]
    r1 = jnp.sum(jnp.where(p1, before, 0.0), axis=1, keepdims=True)
    r2 = jnp.sum(jnp.where(p2, before, 0.0), axis=1, keepdims=True)
    seen_ref[...] += jnp.sum(member, axis=0, keepdims=True)
    cnt_ref[...] = seen_ref[...]
    cols = (i1.astype(F32), i2.astype(F32), r1, r2, 1.0 / den, e2 / den)
    info = jnp.zeros_like(logits)
    for c, v in enumerate(cols):
        info = jnp.where(lane == c, v, info)
    info_ref[...] = info


def _router(x2, w3):
    n, d = x2.shape
    tm = ROW_TILE
    return pl.pallas_call(
        _router_kernel,
        grid=(n // tm,),
        in_specs=[pl.BlockSpec((tm, d), lambda i: (i, 0)),
                  pl.BlockSpec(w3.shape, lambda i: (0, 0))],
        out_specs=[pl.BlockSpec((tm, LANES), lambda i: (i, 0)),
                   pl.BlockSpec((1, LANES), lambda i: (0, 0))],
        out_shape=[jax.ShapeDtypeStruct((n, LANES), F32),
                   jax.ShapeDtypeStruct((1, LANES), F32)],
        scratch_shapes=[pltpu.VMEM((1, LANES), F32)],
        compiler_params=_params(("arbitrary",)),
        name="router",
    )(x2, w3)


def _route_plan(info, counts, n_tiles):
    tmf = MOE_ROW_TILE
    cnt = counts[0, :N_EXPERTS].astype(jnp.int32)
    padded = (cnt + tmf - 1) // tmf * tmf
    ends = jnp.cumsum(padded)
    offs = ends - padded
    eid = info[:, 0:2].astype(jnp.int32)
    rank = info[:, 2:4].astype(jnp.int32)
    pos = rank + jnp.sum(jnp.where(eid[..., None] == jnp.arange(N_EXPERTS), offs, 0), axis=-1)
    n_used = ends[-1] // tmf
    tile = jnp.arange(n_tiles)
    owner = jnp.minimum(jnp.sum(tile[:, None] * tmf >= ends[None, :], axis=1), N_EXPERTS - 1)
    owner = jnp.where(tile < n_used, owner, owner[n_used - 1])
    return (pos.reshape(-1, 1, 2 * ROW_TILE), owner.astype(jnp.int32),
            n_used.reshape(1).astype(jnp.int32))


def _dispatch_kernel(pos_ref, x_ref, zero_ref, xs_ref, sem):
    del zero_ref
    tm = x_ref.shape[0]

    def row_copy(t, s):
        return pltpu.make_async_copy(x_ref.at[pl.ds(t, 1), :],
                                     xs_ref.at[pl.ds(pos_ref[0, 2 * t + s], 1), :], sem)

    def issue(t, c):
        row_copy(t, 0).start()
        row_copy(t, 1).start()
        return c

    lax.fori_loop(0, tm, issue, 0, unroll=8)
    for _ in range(2):
        pltpu.make_async_copy(x_ref, xs_ref.at[pl.ds(0, tm), :], sem).wait()


def _dispatch(x2, pos, n_rows):
    n, d = x2.shape
    tm = ROW_TILE
    return pl.pallas_call(
        _dispatch_kernel,
        grid=(n // tm,),
        in_specs=[pl.BlockSpec((None, 1, 2 * tm), lambda i: (i, 0, 0), memory_space=pltpu.SMEM),
                  pl.BlockSpec((tm, d), lambda i: (i, 0)),
                  pl.BlockSpec(memory_space=pl.ANY)],
        out_specs=pl.BlockSpec(memory_space=pl.ANY),
        out_shape=jax.ShapeDtypeStruct((n_rows, d), F32),
        input_output_aliases={2: 0},
        scratch_shapes=[pltpu.SemaphoreType.DMA],
        compiler_params=_params(("arbitrary",)),
        name="moe_dispatch",
    )(pos, x2, jnp.zeros((n_rows, d), F32))


def _moe_ffn_kernel(owner_ref, used_ref, xs_ref, wg_ref, wu_ref, wd_ref, ys_ref, acc_ref, xb_ref):
    del owner_ref
    i, f = pl.program_id(0), pl.program_id(1)
    used = i < used_ref[0]

    @pl.when(f == 0)
    def _():
        acc_ref[...] = jnp.zeros_like(acc_ref)
        xb_ref[...] = xs_ref[...].astype(BF16)

    @pl.when(used)
    def _():
        acc_ref[...] += _swiglu_chunk(xb_ref[...], wg_ref, wu_ref, wd_ref)

    @pl.when(f == pl.num_programs(1) - 1)
    def _():
        ys_ref[...] = acc_ref[...]


def _moe_ffn(xs, owner, n_used, wg, wu, wd):
    n_rows, d = xs.shape
    tm, tf = MOE_ROW_TILE, FFN_COL_TILE
    ff = wg.shape[2]
    nf = ff // tf
    col = lambda i, f, used: jnp.where(i < used[0], f, nf - 1)
    return pl.pallas_call(
        _moe_ffn_kernel,
        grid_spec=pltpu.PrefetchScalarGridSpec(
            num_scalar_prefetch=2,
            grid=(n_rows // tm, nf),
            in_specs=[
                pl.BlockSpec((tm, d), lambda i, f, own, used: (jnp.minimum(i, used[0] - 1), 0)),
                pl.BlockSpec((None, d, tf), lambda i, f, own, used: (own[i], 0, col(i, f, used))),
                pl.BlockSpec((None, d, tf), lambda i, f, own, used: (own[i], 0, col(i, f, used))),
                pl.BlockSpec((None, tf, d), lambda i, f, own, used: (own[i], col(i, f, used), 0)),
            ],
            out_specs=pl.BlockSpec((tm, d), lambda i, f, own, used: (i, 0)),
            scratch_shapes=[pltpu.VMEM((tm, d), F32), pltpu.VMEM((tm, d), BF16)],
        ),
        out_shape=jax.ShapeDtypeStruct((n_rows, d), F32),
        compiler_params=_params(("arbitrary", "arbitrary")),
        name="moe_ffn",
    )(owner, n_used, xs, wg, wu, wd)


def _combine_ln_kernel(pos_ref, x_ref, info_ref, ys_ref, g_ref, b_ref, o_ref, ybuf_ref, sem):
    tm = x_ref.shape[0]

    def row_copy(t, s):
        return pltpu.make_async_copy(ys_ref.at[pl.ds(pos_ref[0, 2 * t + s], 1), :],
                                     ybuf_ref.at[s, pl.ds(t, 1), :], sem)

    def issue(t, c):
        row_copy(t, 0).start()
        row_copy(t, 1).start()
        return c

    lax.fori_loop(0, tm, issue, 0, unroll=8)
    for s in range(2):
        pltpu.make_async_copy(ys_ref.at[pl.ds(0, tm), :], ybuf_ref.at[s], sem).wait()
    info = info_ref[...]
    lane = lax.broadcasted_iota(jnp.int32, info.shape, 1)
    w1 = jnp.sum(jnp.where(lane == 4, info, 0.0), axis=1, keepdims=True)
    w2 = jnp.sum(jnp.where(lane == 5, info, 0.0), axis=1, keepdims=True)
    y = w1 * ybuf_ref[0] + w2 * ybuf_ref[1]
    o_ref[...] = _ln(ALPHA * x_ref[...] + y, g_ref[...], b_ref[...])


def _combine_ln(x2, info, pos, ys, gamma, beta):
    n, d = x2.shape
    tm = ROW_TILE
    return pl.pallas_call(
        _combine_ln_kernel,
        grid=(n // tm,),
        in_specs=[pl.BlockSpec((None, 1, 2 * tm), lambda i: (i, 0, 0), memory_space=pltpu.SMEM),
                  pl.BlockSpec((tm, d), lambda i: (i, 0)),
                  pl.BlockSpec((tm, LANES), lambda i: (i, 0)),
                  pl.BlockSpec(memory_space=pl.ANY),
                  pl.BlockSpec((1, d), lambda i: (0, 0)),
                  pl.BlockSpec((1, d), lambda i: (0, 0))],
        out_specs=pl.BlockSpec((tm, d), lambda i: (i, 0)),
        out_shape=jax.ShapeDtypeStruct((n, d), F32),
        scratch_shapes=[pltpu.VMEM((2, tm, d), F32), pltpu.SemaphoreType.DMA],
        compiler_params=_params(("arbitrary",)),
        name="moe_combine_ln",
    )(pos, x2, info, ys, gamma, beta)


def _moe_ln(x2, w_router3, wg, wu, wd, gamma, beta):
    n = x2.shape[0]
    n_tiles = 2 * n // MOE_ROW_TILE + N_EXPERTS
    info, counts = _router(x2, w_router3)
    pos, owner, n_used = _route_plan(info, counts, n_tiles)
    xs = _dispatch(x2, pos, n_tiles * MOE_ROW_TILE)
    ys = _moe_ffn(xs, owner, n_used, wg, wu, wd)
    return _combine_ln(x2, info, pos, ys, gamma, beta)


def _pad_cols(w, width):
    return jnp.pad(w, ((0, 0), (0, width - w.shape[1])))


def _even_weights(w_in, b_forget):
    wq, wk, wv, wf, wu = jnp.split(
        w_in, [FOX_WIDTH, 2 * FOX_WIDTH, 3 * FOX_WIDTH, 3 * FOX_WIDTH + FOX_HEADS], axis=1)
    w_qk = jnp.concatenate([wq * QK_SCALE, wk], axis=1).astype(BF16)
    w_vt = wv.T.astype(BF16)
    w_f3 = _pad_cols(jnp.concatenate([wf, wf, wf], axis=1), LANES).astype(BF16)
    b_f3 = _pad_cols(jnp.concatenate([b_forget, b_forget, b_forget])[None, :], LANES)
    return w_qk, w_vt, w_f3, b_f3, wu.astype(BF16)


def _odd_weights(w_in):
    wq, wk, wv = jnp.split(w_in, 3, axis=1)
    return jnp.concatenate([wq * QK_SCALE, wk], axis=1).astype(BF16), wv.T.astype(BF16)


def _router_weights(w_router):
    return jnp.concatenate([_pad_cols(p.astype(F32), LANES).astype(BF16)
                            for p in _split3(w_router)], axis=1)


def kernel(x, ln1_g, ln1_b, ln2_g, ln2_b, ev_w_in, ev_b_forget, ev_pool_w, ev_pool_scale, ev_w_out,
           ev_ffn_gate, ev_ffn_up, ev_ffn_down, od_w_in, od_w_out, od_router, od_exp_gate, od_exp_up,
           od_exp_down):
    bsz, seq, d = x.shape
    n = bsz * seq
    x2 = x.reshape(n, d)
    for layer in range(DEPTH):
        li = layer // 2
        g1, b1 = ln1_g[layer][None, :], ln1_b[layer][None, :]
        g2, b2 = ln2_g[layer][None, :], ln2_b[layer][None, :]
        if layer % 2 == 0:
            w_qk, w_vt, w_f3, b_f3, w_u = _even_weights(ev_w_in[li], ev_b_forget[li])
            qk, vt, z, u = _proj(x2, seq, [(w_qk, "nn", BF16), (w_vt, "nt", BF16),
                                           (w_f3, "nn", F32), (w_u, "nn", F32)])
            cum3 = _fox_cum(z.reshape(bsz, seq, LANES), b_f3)
            att = _fox_attn(qk.reshape(bsz, seq, -1), cum3, vt)
            pooled = _pool(u.reshape(bsz, seq, POOL_WIDTH), ev_pool_w[li].astype(BF16),
                           ev_pool_scale[li][None, :])
            w_out = ev_w_out[li].astype(BF16)
            x2 = _outproj_ln([att.reshape(n, FOX_WIDTH), pooled.reshape(n, POOL_WIDTH)],
                             [w_out[:FOX_WIDTH], w_out[FOX_WIDTH:]], x2, g1, b1)
            x2 = _ffn_ln(x2, ev_ffn_gate[li].astype(BF16), ev_ffn_up[li].astype(BF16),
                         ev_ffn_down[li].astype(BF16), g2, b2)
        else:
            w_qk, w_vt = _odd_weights(od_w_in[li])
            qk, vt = _proj(x2, seq, [(w_qk, "nn", BF16), (w_vt, "nt", BF16)])
            att = _moba_attn(qk.reshape(bsz, seq, -1), vt)
            x2 = _outproj_ln([att.reshape(n, MOBA_WIDTH)], [od_w_out[li].astype(BF16)], x2, g1, b1)
            x2 = _moe_ln(x2, _router_weights(od_router[li]), od_exp_gate[li].astype(BF16),
                         od_exp_up[li].astype(BF16), od_exp_down[li].astype(BF16), g2, b2)
    return x2.reshape(bsz, seq, d)
```

```python
import functools

import jax
import jax.numpy as jnp
from jax import lax
from jax.experimental import pallas as pl
from jax.experimental.pallas import tpu as pltpu

F32 = jnp.float32
BF16 = jnp.bfloat16

D_MODEL = 1024
DEPTH = 4
HEAD_DIM = 64
FOX_HEADS = 8
FOX_WIDTH = FOX_HEADS * HEAD_DIM
POOL_WINDOWS = (2, 4, 8, 16)
POOL_WIDTH = 512
POOL_GROUP_DIM = 128
MOBA_HEADS = 16
MOBA_WIDTH = MOBA_HEADS * HEAD_DIM
MOBA_BLOCK = 256
MOBA_TOPK = 3
D_FF = 3584
N_EXPERTS = 8
LN_EPS = 1e-5
ALPHA = (2 * DEPTH) ** 0.25
LOG2E = 1.4426950408889634
Q_LOG2_SCALE = HEAD_DIM ** -0.5 * LOG2E

LANES = 128
ATT_TILE = 512
ROW_TILE = 512
FFN_ROW_TILE = 1024
FFN_COL_TILE = 512
MOE_ROW_TILE = 512
CUM_BLOCK = 256
NEG_BIG = -1e30
VMEM_LIMIT = 48 * 1024 * 1024

_NT = (((1,), (1,)), ((), ()))


def _params(sem):
    return pltpu.CompilerParams(dimension_semantics=sem, vmem_limit_bytes=VMEM_LIMIT)


def _split3(x):
    hi = x.astype(BF16)
    r = x - hi.astype(F32)
    mid = r.astype(BF16)
    lo = (r - mid.astype(F32)).astype(BF16)
    return hi, mid, lo


def _ln(y, g, b):
    mu = jnp.mean(y, axis=-1, keepdims=True)
    yc = y - mu
    var = jnp.mean(yc * yc, axis=-1, keepdims=True)
    return yc * lax.rsqrt(var + LN_EPS) * g + b


def _proj_kernel(*refs, kinds):
    x_ref = refs[0]
    w_refs = refs[1:1 + len(kinds)]
    o_refs = refs[1 + len(kinds):]
    xb = x_ref[...].astype(BF16)
    for kind, w_ref, o_ref in zip(kinds, w_refs, o_refs):
        if kind == "nt":
            m = w_ref.shape[0]
            for c0 in range(0, m, 512):
                c1 = min(c0 + 512, m)
                o_ref[0, 0, c0:c1, :] = lax.dot_general(
                    w_ref[c0:c1, :], xb, _NT, preferred_element_type=F32).astype(o_ref.dtype)
        else:
            m = w_ref.shape[1]
            for c0 in range(0, m, 512):
                c1 = min(c0 + 512, m)
                y = jnp.dot(xb, w_ref[:, c0:c1], preferred_element_type=F32)
                if kind == "qk" and c0 < m // 2:
                    y = y * Q_LOG2_SCALE
                o_ref[:, c0:c1] = y.astype(o_ref.dtype)


def _proj(x2, seq, specs):
    n, d = x2.shape
    tm = ROW_TILE
    tps = seq // tm
    kinds = tuple(k for _, k, _ in specs)
    in_specs = [pl.BlockSpec((tm, d), lambda i: (i, 0))]
    out_specs, out_shapes = [], []
    for w, kind, dt in specs:
        in_specs.append(pl.BlockSpec(w.shape, lambda i: (0, 0)))
        if kind == "nt":
            m = w.shape[0]
            out_shapes.append(jax.ShapeDtypeStruct((n // seq, tps, m, tm), dt))
            out_specs.append(pl.BlockSpec((1, 1, m, tm), lambda i: (i // tps, i % tps, 0, 0)))
        else:
            m = w.shape[1]
            out_shapes.append(jax.ShapeDtypeStruct((n, m), dt))
            out_specs.append(pl.BlockSpec((tm, m), lambda i: (i, 0)))
    return pl.pallas_call(
        functools.partial(_proj_kernel, kinds=kinds),
        grid=(n // tm,),
        in_specs=in_specs,
        out_specs=out_specs,
        out_shape=out_shapes,
        compiler_params=_params(("arbitrary",)),
        name="proj",
    )(x2, *[w for w, _, _ in specs])


def _fox_cum_kernel(z_ref, b_ref, o_ref):
    seq = z_ref.shape[1]
    blk = CUM_BLOCK
    row = lax.broadcasted_iota(jnp.int32, (blk, blk), 0)
    col = lax.broadcasted_iota(jnp.int32, (blk, blk), 1)
    tri = (col <= row).astype(BF16)
    lane = lax.broadcasted_iota(jnp.int32, (blk, LANES), 1)
    carry = jnp.zeros((1, LANES), F32)
    for t in range(seq // blk):
        logf = jax.nn.log_sigmoid(z_ref[0, t * blk:(t + 1) * blk, :] + b_ref[...])
        hi, mid, lo = _split3(logf)
        c = (jnp.dot(tri, hi, preferred_element_type=F32)
             + jnp.dot(tri, mid, preferred_element_type=F32)
             + jnp.dot(tri, lo, preferred_element_type=F32)) + carry
        carry = c[blk - 1:blk, :]
        chi, cmid, clo = _split3(c * LOG2E)
        o_ref[0, t * blk:(t + 1) * blk, :] = jnp.where(
            lane < FOX_HEADS, chi, jnp.where(lane < 2 * FOX_HEADS, cmid, clo))


def _fox_cum(z3, bias3):
    b, s, _ = z3.shape
    return pl.pallas_call(
        _fox_cum_kernel,
        grid=(b,),
        in_specs=[pl.BlockSpec((1, s, LANES), lambda i: (i, 0, 0)),
                  pl.BlockSpec((1, LANES), lambda i: (0, 0))],
        out_specs=pl.BlockSpec((1, s, LANES), lambda i: (i, 0, 0)),
        out_shape=jax.ShapeDtypeStruct((b, s, LANES), BF16),
        compiler_params=_params(("arbitrary",)),
        name="fox_cum",
    )(z3, bias3)


def _softmax_step(s, vt, m, l, acc):
    m_new = jnp.maximum(m, jnp.max(s, axis=0, keepdims=True))
    alpha = jnp.exp2(m - m_new)
    p = jnp.exp2(s - m_new)
    l_new = alpha * l + jnp.sum(p, axis=0, keepdims=True)
    acc_new = alpha * acc + jnp.dot(vt, p.astype(BF16), preferred_element_type=F32)
    return m_new, l_new, acc_new


def _attn_scratch(t):
    return [pltpu.VMEM((2, 2, t, t), F32), pltpu.VMEM((2, 1, t), F32),
            pltpu.VMEM((2, 1, t), F32), pltpu.VMEM((2, HEAD_DIM, t), F32)]


def _attn_pipeline(i, ntile, scores, vt_ref, o_ref, s_ref, m_ref, l_ref, acc_ref):
    tq = s_ref.shape[3]
    causal = (lax.broadcasted_iota(jnp.int32, (tq, tq), 0)
              <= lax.broadcasted_iota(jnp.int32, (tq, tq), 1))

    def produce(buf, n, diagonal=False):
        for hh, s in enumerate(scores(n)):
            s_ref[buf, hh] = jnp.where(causal, s, -jnp.inf) if diagonal else s

    def consume(buf, n):
        for hh in range(2):
            vt = vt_ref[0, n, HEAD_DIM * hh:HEAD_DIM * (hh + 1), :]
            m_ref[hh], l_ref[hh], acc_ref[hh] = _softmax_step(
                s_ref[buf, hh], vt, m_ref[hh], l_ref[hh], acc_ref[hh])

    m_ref[...] = jnp.full(m_ref.shape, -jnp.inf, F32)
    l_ref[...] = jnp.zeros(l_ref.shape, F32)
    acc_ref[...] = jnp.zeros(acc_ref.shape, F32)
    produce(0, i, diagonal=True)
    count = i + 1

    def pair(a, c):
        produce(1, 2 * a)
        consume(0, jnp.where(a == 0, i, 2 * a - 1))
        produce(0, jnp.minimum(2 * a + 1, ntile - 1))
        consume(1, 2 * a)
        return c

    lax.fori_loop(0, count // 2, pair, 0)

    @pl.when(count % 2 == 1)
    def _():
        consume(0, jnp.where(i == 0, i, i - 1))

    o = jnp.concatenate([acc_ref[hh] / l_ref[hh] for hh in range(2)], axis=0)
    o_ref[0] = o.T.astype(o_ref.dtype)


def _fox_attn_kernel(q_ref, k_ref, cq_ref, ck_ref, vt_ref, o_ref, kaug_ref, s_ref, m_ref, l_ref, acc_ref):
    tq = q_ref.shape[1]
    ntile = k_ref.shape[1] // tq
    j = pl.program_id(1)
    i = pl.program_id(2)
    lane = lax.broadcasted_iota(jnp.int32, (1, LANES), 1)
    prow = lax.broadcasted_iota(jnp.int32, (LANES, LANES), 0)
    pcol = lax.broadcasted_iota(jnp.int32, (LANES, LANES), 1)

    def place(h, first_lane, sign):
        hit = (((prow == h) & (pcol == first_lane))
               | ((prow == h + FOX_HEADS) & (pcol == first_lane + 1))
               | ((prow == h + 2 * FOX_HEADS) & (pcol == first_lane + 2)))
        return jnp.where(hit, sign, 0.0).astype(BF16)

    def augment(x, c3, h, hh, gate_off, sign, ones_off):
        base = HEAD_DIM * (1 - hh)
        head_lanes = (lane >= HEAD_DIM * hh) & (lane < HEAD_DIM * (hh + 1))
        ones = (lane >= base + ones_off) & (lane < base + ones_off + 3)
        aug = jnp.dot(c3, place(h, base + gate_off, sign), preferred_element_type=F32)
        return jnp.where(head_lanes, x, jnp.where(ones, 1.0, aug).astype(BF16))

    @pl.when(i == 0)
    def _():
        for hh in range(2):
            for t in range(ntile):
                kaug_ref[hh, t] = augment(k_ref[0, t * tq:(t + 1) * tq, :],
                                          ck_ref[0, t * tq:(t + 1) * tq, :],
                                          2 * j + hh, hh, 0, -1.0, 3)

    qaug = [augment(q_ref[0], cq_ref[0], 2 * j + hh, hh, 3, 1.0, 0) for hh in range(2)]

    def scores(n):
        return [lax.dot_general(kaug_ref[hh, n], qaug[hh], _NT, preferred_element_type=F32)
                for hh in range(2)]

    _attn_pipeline(i, ntile, scores, vt_ref, o_ref, s_ref, m_ref, l_ref, acc_ref)


def _fox_attn(qk3, cum3, vt4):
    b, s, _ = qk3.shape
    t = ATT_TILE
    npair = FOX_HEADS // 2
    return pl.pallas_call(
        _fox_attn_kernel,
        grid=(b, npair, s // t),
        in_specs=[
            pl.BlockSpec((1, t, LANES), lambda b_, j, i: (b_, i, j)),
            pl.BlockSpec((1, s, LANES), lambda b_, j, i: (b_, 0, npair + j)),
            pl.BlockSpec((1, t, LANES), lambda b_, j, i: (b_, i, 0)),
            pl.BlockSpec((1, s, LANES), lambda b_, j, i: (b_, 0, 0)),
            pl.BlockSpec((1, s // t, LANES, t), lambda b_, j, i: (b_, 0, j, 0)),
        ],
        out_specs=pl.BlockSpec((1, t, LANES), lambda b_, j, i: (b_, i, j)),
        out_shape=jax.ShapeDtypeStruct((b, s, FOX_WIDTH), BF16),
        scratch_shapes=[pltpu.VMEM((2, s // t, t, LANES), BF16)] + _attn_scratch(t),
        compiler_params=_params(("arbitrary", "arbitrary", "arbitrary")),
        name="fox_attn",
    )(qk3, qk3, cum3, cum3, vt4)


def _moba_attn_kernel(q_ref, k_ref, vt_ref, o_ref, km_ref, kaug_ref, s_ref, m_ref, l_ref, acc_ref):
    tq = q_ref.shape[1]
    seq = k_ref.shape[1]
    nb = seq // MOBA_BLOCK
    nbp = km_ref.shape[1] // 3
    bpt = tq // MOBA_BLOCK
    blk_shift = MOBA_BLOCK.bit_length() - 1
    assert nbp <= HEAD_DIM
    i = pl.program_id(2)
    lane = lax.broadcasted_iota(jnp.int32, (1, LANES), 1)

    @pl.when(i == 0)
    def _():
        rows = [jnp.sum(k_ref[0, n * MOBA_BLOCK:(n + 1) * MOBA_BLOCK, :].astype(F32),
                        axis=0, keepdims=True) * (1.0 / MOBA_BLOCK) for n in range(nb)]
        if nbp > nb:
            rows.append(jnp.zeros((nbp - nb, LANES), F32))
        kmean = jnp.concatenate(rows, axis=0)
        key_blk = lax.shift_right_logical(lax.broadcasted_iota(jnp.int32, (tq, LANES), 0), blk_shift)
        lane2 = lax.broadcasted_iota(jnp.int32, (tq, LANES), 1)
        for hh in range(2):
            head_lanes = (lane >= HEAD_DIM * hh) & (lane < HEAD_DIM * (hh + 1))
            km_ref[hh] = jnp.concatenate(_split3(jnp.where(head_lanes, kmean, 0.0)), axis=0)
            for t in range(seq // tq):
                onehot = jnp.where(lane2 - HEAD_DIM * (1 - hh) == key_blk + t * bpt, 1.0, 0.0)
                kaug_ref[hh, t] = jnp.where(head_lanes, k_ref[0, t * tq:(t + 1) * tq, :],
                                            onehot.astype(BF16))

    q = q_ref[0]
    nidx = lax.broadcasted_iota(jnp.int32, (nbp, tq), 0)
    qblk = lax.shift_right_logical(i * tq + lax.broadcasted_iota(jnp.int32, (nbp, tq), 1), blk_shift)
    past = nidx < qblk
    qaug = []
    for hh in range(2):
        head_lanes = (lane >= HEAD_DIM * hh) & (lane < HEAD_DIM * (hh + 1))
        qh = jnp.where(head_lanes, q, jnp.zeros_like(q))
        g3 = lax.dot_general(km_ref[hh], qh, _NT, preferred_element_type=F32)
        gate = g3[0:nbp] + g3[nbp:2 * nbp] + g3[2 * nbp:3 * nbp]
        g = jnp.where(past, gate, -jnp.inf)
        sel = nidx == qblk
        for _ in range(MOBA_TOPK):
            first = jnp.min(jnp.where(g == jnp.max(g, axis=0, keepdims=True), nidx, nbp),
                            axis=0, keepdims=True)
            pick = nidx == first
            sel = sel | (pick & past)
            g = jnp.where(pick, -jnp.inf, g)
        base = HEAD_DIM * (1 - hh)
        rows = [jnp.zeros((base, tq), F32), jnp.where(sel, 0.0, NEG_BIG),
                jnp.zeros((LANES - base - nbp, tq), F32)]
        bias = jnp.concatenate([r for r in rows if r.shape[0]], axis=0)
        qaug.append(jnp.where(head_lanes, q, bias.T.astype(BF16)))

    def scores(n):
        return [lax.dot_general(kaug_ref[hh, n], qaug[hh], _NT, preferred_element_type=F32)
                for hh in range(2)]

    _attn_pipeline(i, seq // tq, scores, vt_ref, o_ref, s_ref, m_ref, l_ref, acc_ref)


def _moba_attn(qk3, vt4):
    b, s, _ = qk3.shape
    t = ATT_TILE
    npair = MOBA_HEADS // 2
    nb = s // MOBA_BLOCK
    nbp = -(-nb // 16) * 16
    return pl.pallas_call(
        _moba_attn_kernel,
        grid=(b, npair, s // t),
        in_specs=[
            pl.BlockSpec((1, t, LANES), lambda b_, j, i: (b_, i, j)),
            pl.BlockSpec((1, s, LANES), lambda b_, j, i: (b_, 0, npair + j)),
            pl.BlockSpec((1, s // t, LANES, t), lambda b_, j, i: (b_, 0, j, 0)),
        ],
        out_specs=pl.BlockSpec((1, t, LANES), lambda b_, j, i: (b_, i, j)),
        out_shape=jax.ShapeDtypeStruct((b, s, MOBA_WIDTH), BF16),
        scratch_shapes=[pltpu.VMEM((2, 3 * nbp, LANES), BF16),
                        pltpu.VMEM((2, s // t, t, LANES), BF16)] + _attn_scratch(t),
        compiler_params=_params(("arbitrary", "arbitrary", "arbitrary")),
        name="moba_attn",
    )(qk3, qk3, vt4)


def _pool_kernel(u_ref, w_ref, sc_ref, o_ref):
    g = pl.program_id(1)
    u = u_ref[0]
    t = lax.broadcasted_iota(jnp.int32, u.shape, 0)
    wsum = u
    acc = jnp.zeros_like(u)
    for r, w in enumerate(POOL_WINDOWS):
        d = w // 2
        wsum = wsum + jnp.where(t >= d, pltpu.roll(wsum, d, 0), 0.0)
        cnt = jnp.minimum(t + 1, w).astype(F32)
        acc = jnp.where(g == r, wsum / cnt, acc)
    mixed = (acc - u).astype(BF16)
    y = jnp.dot(mixed, w_ref[0], preferred_element_type=F32) * sc_ref[...]
    o_ref[0] = y.astype(o_ref.dtype)


def _pool(u3, pool_w, pool_scale):
    b, s, _ = u3.shape
    ng = len(POOL_WINDOWS)
    return pl.pallas_call(
        _pool_kernel,
        grid=(b, ng),
        in_specs=[pl.BlockSpec((1, s, LANES), lambda b_, g: (b_, 0, g)),
                  pl.BlockSpec((1, POOL_GROUP_DIM, POOL_GROUP_DIM), lambda b_, g: (g, 0, 0)),
                  pl.BlockSpec((1, LANES), lambda b_, g: (0, g))],
        out_specs=pl.BlockSpec((1, s, LANES), lambda b_, g: (b_, 0, g)),
        out_shape=jax.ShapeDtypeStruct((b, s, POOL_WIDTH), BF16),
        compiler_params=_params(("arbitrary", "arbitrary")),
        name="pool",
    )(u3, pool_w, pool_scale)


def _outproj_ln_kernel(*refs, n_in):
    a_refs = refs[:n_in]
    w_refs = refs[n_in:2 * n_in]
    x_ref, g_ref, b_ref, o_ref = refs[2 * n_in:]
    mix = jnp.dot(a_refs[0][...], w_refs[0][...], preferred_element_type=F32)
    for a_ref, w_ref in zip(a_refs[1:], w_refs[1:]):
        mix = mix + jnp.dot(a_ref[...], w_ref[...], preferred_element_type=F32)
    o_ref[...] = _ln(ALPHA * x_ref[...] + mix, g_ref[...], b_ref[...])


def _outproj_ln(acts, weights, x2, gamma, beta):
    n, d = x2.shape
    tm = ROW_TILE
    in_specs = [pl.BlockSpec((tm, a.shape[1]), lambda i: (i, 0)) for a in acts]
    in_specs += [pl.BlockSpec(w.shape, lambda i: (0, 0)) for w in weights]
    in_specs += [pl.BlockSpec((tm, d), lambda i: (i, 0)),
                 pl.BlockSpec((1, d), lambda i: (0, 0)),
                 pl.BlockSpec((1, d), lambda i: (0, 0))]
    return pl.pallas_call(
        functools.partial(_outproj_ln_kernel, n_in=len(acts)),
        grid=(n // tm,),
        in_specs=in_specs,
        out_specs=pl.BlockSpec((tm, d), lambda i: (i, 0)),
        out_shape=jax.ShapeDtypeStruct((n, d), F32),
        compiler_params=_params(("arbitrary",)),
        name="outproj_ln",
    )(*acts, *weights, x2, gamma, beta)


def _swiglu_chunk(xb, wg_ref, wu_ref, wd_ref):
    hid = (jax.nn.silu(jnp.dot(xb, wg_ref[...], preferred_element_type=F32))
           * jnp.dot(xb, wu_ref[...], preferred_element_type=F32))
    return jnp.dot(hid.astype(BF16), wd_ref[...], preferred_element_type=F32)


def _ffn_kernel(x_ref, wg_ref, wu_ref, wd_ref, g_ref, b_ref, o_ref, acc_ref, xb_ref):
    f = pl.program_id(1)

    @pl.when(f == 0)
    def _():
        acc_ref[...] = jnp.zeros_like(acc_ref)
        xb_ref[...] = x_ref[...].astype(BF16)

    acc_ref[...] += _swiglu_chunk(xb_ref[...], wg_ref, wu_ref, wd_ref)

    @pl.when(f == pl.num_programs(1) - 1)
    def _():
        o_ref[...] = _ln(ALPHA * x_ref[...] + acc_ref[...], g_ref[...], b_ref[...])


def _ffn_ln(x2, wg, wu, wd, gamma, beta):
    n, d = x2.shape
    tm, tf = FFN_ROW_TILE, FFN_COL_TILE
    ff = wg.shape[1]
    return pl.pallas_call(
        _ffn_kernel,
        grid=(n // tm, ff // tf),
        in_specs=[pl.BlockSpec((tm, d), lambda i, f: (i, 0)),
                  pl.BlockSpec((d, tf), lambda i, f: (0, f)),
                  pl.BlockSpec((d, tf), lambda i, f: (0, f)),
                  pl.BlockSpec((tf, d), lambda i, f: (f, 0)),
                  pl.BlockSpec((1, d), lambda i, f: (0, 0)),
                  pl.BlockSpec((1, d), lambda i, f: (0, 0))],
        out_specs=pl.BlockSpec((tm, d), lambda i, f: (i, 0)),
        out_shape=jax.ShapeDtypeStruct((n, d), F32),
        scratch_shapes=[pltpu.VMEM((tm, d), F32), pltpu.VMEM((tm, d), BF16)],
        compiler_params=_params(("arbitrary", "arbitrary")),
        name="ffn_ln",
    )(x2, wg, wu, wd, gamma, beta)


def _router_kernel(x_ref, w_ref, info_ref, cnt_ref, seen_ref):
    @pl.when(pl.program_id(0) == 0)
    def _():
        seen_ref[...] = jnp.zeros_like(seen_ref)

    x = x_ref[...]
    tm = x.shape[0]
    xh = x.astype(BF16)
    xm = (x - xh.astype(F32)).astype(BF16)
    l3 = (jnp.dot(xh, w_ref[...], preferred_element_type=F32)
          + jnp.dot(xm, w_ref[...], preferred_element_type=F32))
    logits = l3[:, 0:LANES] + l3[:, LANES:2 * LANES] + l3[:, 2 * LANES:3 * LANES]
    lane = lax.broadcasted_iota(jnp.int32, logits.shape, 1)
    g = jnp.where(lane < N_EXPERTS, logits, -jnp.inf)
    v1 = jnp.max(g, axis=1, keepdims=True)
    i1 = jnp.min(jnp.where(g == v1, lane, LANES), axis=1, keepdims=True)
    p1 = lane == i1
    g = jnp.where(p1, -jnp.inf, g)
    v2 = jnp.max(g, axis=1, keepdims=True)
    i2 = jnp.min(jnp.where(g == v2, lane, LANES), axis=1, keepdims=True)
    p2 = lane == i2
    e2 = jnp.exp(v2 - v1)
    den = 1.0 + e2
    member = jnp.where(p1 | p2, 1.0, 0.0)
    earlier = (lax.broadcasted_iota(jnp.int32, (tm, tm), 1)
               < lax.broadcasted_iota(jnp.int32, (tm, tm), 0)).astype(BF16)
    before = jnp.dot(earlier, member.astype(BF16), preferred_element_type=F32) + seen_ref[...]
    r1 = jnp.sum(jnp.where(p1, before, 0.0), axis=1, keepdims=True)
    r2 = jnp.sum(jnp.where(p2, before, 0.0), axis=1, keepdims=True)
    seen_ref[...] += jnp.sum(member, axis=0, keepdims=True)
    cnt_ref[...] = seen_ref[...]
    cols = (i1.astype(F32), i2.astype(F32), r1, r2, 1.0 / den, e2 / den)
    info = jnp.zeros_like(logits)
    for c, v in enumerate(cols):
        info = jnp.where(lane == c, v, info)
    info_ref[...] = info


def _router(x2, w3):
    n, d = x2.shape
    tm = ROW_TILE
    return pl.pallas_call(
        _router_kernel,
        grid=(n // tm,),
        in_specs=[pl.BlockSpec((tm, d), lambda i: (i, 0)),
                  pl.BlockSpec(w3.shape, lambda i: (0, 0))],
        out_specs=[pl.BlockSpec((tm, LANES), lambda i: (i, 0)),
                   pl.BlockSpec((1, LANES), lambda i: (0, 0))],
        out_shape=[jax.ShapeDtypeStruct((n, LANES), F32),
                   jax.ShapeDtypeStruct((1, LANES), F32)],
        scratch_shapes=[pltpu.VMEM((1, LANES), F32)],
        compiler_params=_params(("arbitrary",)),
        name="router",
    )(x2, w3)


def _route_plan(info, counts, n_tiles):
    tmf = MOE_ROW_TILE
    cnt = counts[0, :N_EXPERTS].astype(jnp.int32)
    padded = (cnt + tmf - 1) // tmf * tmf
    ends = jnp.cumsum(padded)
    offs = ends - padded
    eid = info[:, 0:2].astype(jnp.int32)
    rank = info[:, 2:4].astype(jnp.int32)
    pos = rank + jnp.sum(jnp.where(eid[..., None] == jnp.arange(N_EXPERTS), offs, 0), axis=-1)
    n_used = ends[-1] // tmf
    tile = jnp.arange(n_tiles)
    owner = jnp.minimum(jnp.sum(tile[:, None] * tmf >= ends[None, :], axis=1), N_EXPERTS - 1)
    owner = jnp.where(tile < n_used, owner, owner[n_used - 1])
    return (pos.reshape(-1, 1, 2 * ROW_TILE), owner.astype(jnp.int32),
            n_used.reshape(1).astype(jnp.int32))


def _dispatch_kernel(pos_ref, x_ref, zero_ref, xs_ref, sem):
    del zero_ref
    tm = x_ref.shape[0]

    def row_copy(t, s):
        return pltpu.make_async_copy(x_ref.at[pl.ds(t, 1), :],
                                     xs_ref.at[pl.ds(pos_ref[0, 2 * t + s], 1), :], sem)

    def issue(t, c):
        row_copy(t, 0).start()
        row_copy(t, 1).start()
        return c

    lax.fori_loop(0, tm, issue, 0, unroll=8)
    for _ in range(2):
        pltpu.make_async_copy(x_ref, xs_ref.at[pl.ds(0, tm), :], sem).wait()


def _dispatch(x2, pos, n_rows):
    n, d = x2.shape
    tm = ROW_TILE
    return pl.pallas_call(
        _dispatch_kernel,
        grid=(n // tm,),
        in_specs=[pl.BlockSpec((None, 1, 2 * tm), lambda i: (i, 0, 0), memory_space=pltpu.SMEM),
                  pl.BlockSpec((tm, d), lambda i: (i, 0)),
                  pl.BlockSpec(memory_space=pl.ANY)],
        out_specs=pl.BlockSpec(memory_space=pl.ANY),
        out_shape=jax.ShapeDtypeStruct((n_rows, d), F32),
        input_output_aliases={2: 0},
        scratch_shapes=[pltpu.SemaphoreType.DMA],
        compiler_params=_params(("arbitrary",)),
        name="moe_dispatch",
    )(pos, x2, jnp.zeros((n_rows, d), F32))


def _moe_ffn_kernel(owner_ref, used_ref, xs_ref, wg_ref, wu_ref, wd_ref, ys_ref, acc_ref, xb_ref):
    del owner_ref
    i, f = pl.program_id(0), pl.program_id(1)
    used = i < used_ref[0]

    @pl.when(f == 0)
    def _():
        acc_ref[...] = jnp.zeros_like(acc_ref)
        xb_ref[...] = xs_ref[...].astype(BF16)

    @pl.when(used)
    def _():
        acc_ref[...] += _swiglu_chunk(xb_ref[...], wg_ref, wu_ref, wd_ref)

    @pl.when(f == pl.num_programs(1) - 1)
    def _():
        ys_ref[...] = acc_ref[...]


def _moe_ffn(xs, owner, n_used, wg, wu, wd):
    n_rows, d = xs.shape
    tm, tf = MOE_ROW_TILE, FFN_COL_TILE
    ff = wg.shape[2]
    nf = ff // tf
    col = lambda i, f, used: jnp.where(i < used[0], f, nf - 1)
    return pl.pallas_call(
        _moe_ffn_kernel,
        grid_spec=pltpu.PrefetchScalarGridSpec(
            num_scalar_prefetch=2,
            grid=(n_rows // tm, nf),
            in_specs=[
                pl.BlockSpec((tm, d), lambda i, f, own, used: (jnp.minimum(i, used[0] - 1), 0)),
                pl.BlockSpec((None, d, tf), lambda i, f, own, used: (own[i], 0, col(i, f, used))),
                pl.BlockSpec((None, d, tf), lambda i, f, own, used: (own[i], 0, col(i, f, used))),
                pl.BlockSpec((None, tf, d), lambda i, f, own, used: (own[i], col(i, f, used), 0)),
            ],
            out_specs=pl.BlockSpec((tm, d), lambda i, f, own, used: (i, 0)),
            scratch_shapes=[pltpu.VMEM((tm, d), F32), pltpu.VMEM((tm, d), BF16)],
        ),
        out_shape=jax.ShapeDtypeStruct((n_rows, d), F32),
        compiler_params=_params(("arbitrary", "arbitrary")),
        name="moe_ffn",
    )(owner, n_used, xs, wg, wu, wd)


def _combine_ln_kernel(pos_ref, x_ref, info_ref, ys_ref, g_ref, b_ref, o_ref, ybuf_ref, sem):
    tm = x_ref.shape[0]

    def row_copy(t, s):
        return pltpu.make_async_copy(ys_ref.at[pl.ds(pos_ref[0, 2 * t + s], 1), :],
                                     ybuf_ref.at[s, pl.ds(t, 1), :], sem)

    def issue(t, c):
        row_copy(t, 0).start()
        row_copy(t, 1).start()
        return c

    lax.fori_loop(0, tm, issue, 0, unroll=8)
    for s in range(2):
        pltpu.make_async_copy(ys_ref.at[pl.ds(0, tm), :], ybuf_ref.at[s], sem).wait()
    info = info_ref[...]
    lane = lax.broadcasted_iota(jnp.int32, info.shape, 1)
    w1 = jnp.sum(jnp.where(lane == 4, info, 0.0), axis=1, keepdims=True)
    w2 = jnp.sum(jnp.where(lane == 5, info, 0.0), axis=1, keepdims=True)
    y = w1 * ybuf_ref[0] + w2 * ybuf_ref[1]
    o_ref[...] = _ln(ALPHA * x_ref[...] + y, g_ref[...], b_ref[...])


def _combine_ln(x2, info, pos, ys, gamma, beta):
    n, d = x2.shape
    tm = ROW_TILE
    return pl.pallas_call(
        _combine_ln_kernel,
        grid=(n // tm,),
        in_specs=[pl.BlockSpec((None, 1, 2 * tm), lambda i: (i, 0, 0), memory_space=pltpu.SMEM),
                  pl.BlockSpec((tm, d), lambda i: (i, 0)),
                  pl.BlockSpec((tm, LANES), lambda i: (i, 0)),
                  pl.BlockSpec(memory_space=pl.ANY),
                  pl.BlockSpec((1, d), lambda i: (0, 0)),
                  pl.BlockSpec((1, d), lambda i: (0, 0))],
        out_specs=pl.BlockSpec((tm, d), lambda i: (i, 0)),
        out_shape=jax.ShapeDtypeStruct((n, d), F32),
        scratch_shapes=[pltpu.VMEM((2, tm, d), F32), pltpu.SemaphoreType.DMA],
        compiler_params=_params(("arbitrary",)),
        name="moe_combine_ln",
    )(pos, x2, info, ys, gamma, beta)


def _moe_ln(x2, w_router3, wg, wu, wd, gamma, beta):
    n = x2.shape[0]
    n_tiles = 2 * n // MOE_ROW_TILE + N_EXPERTS
    info, counts = _router(x2, w_router3)
    pos, owner, n_used = _route_plan(info, counts, n_tiles)
    xs = _dispatch(x2, pos, n_tiles * MOE_ROW_TILE)
    ys = _moe_ffn(xs, owner, n_used, wg, wu, wd)
    return _combine_ln(x2, info, pos, ys, gamma, beta)


def _pad_cols(w, width):
    return jnp.pad(w, ((0, 0), (0, width - w.shape[1])))


def _even_weights(w_in, b_forget):
    wq, wk, wv, wf, wu = jnp.split(
        w_in, [FOX_WIDTH, 2 * FOX_WIDTH, 3 * FOX_WIDTH, 3 * FOX_WIDTH + FOX_HEADS], axis=1)
    w_qk = jnp.concatenate([wq, wk], axis=1).astype(BF16)
    w_vt = wv.T.astype(BF16)
    w_f3 = _pad_cols(jnp.concatenate([wf, wf, wf], axis=1), LANES).astype(BF16)
    b_f3 = _pad_cols(jnp.concatenate([b_forget, b_forget, b_forget])[None, :], LANES)
    return w_qk, w_vt, w_f3, b_f3, wu.astype(BF16)


def _odd_weights(w_in):
    wq, wk, wv = jnp.split(w_in, 3, axis=1)
    return jnp.concatenate([wq, wk], axis=1).astype(BF16), wv.T.astype(BF16)


def _router_weights(w_router):
    return jnp.concatenate([_pad_cols(p.astype(F32), LANES).astype(BF16)
                            for p in _split3(w_router)], axis=1)


def kernel(x, ln1_g, ln1_b, ln2_g, ln2_b, ev_w_in, ev_b_forget, ev_pool_w, ev_pool_scale, ev_w_out,
           ev_ffn_gate, ev_ffn_up, ev_ffn_down, od_w_in, od_w_out, od_router, od_exp_gate, od_exp_up,
           od_exp_down):
    bsz, seq, d = x.shape
    n = bsz * seq
    x2 = x.reshape(n, d)
    for layer in range(DEPTH):
        li = layer // 2
        g1, b1 = ln1_g[layer][None, :], ln1_b[layer][None, :]
        g2, b2 = ln2_g[layer][None, :], ln2_b[layer][None, :]
        if layer % 2 == 0:
            w_qk, w_vt, w_f3, b_f3, w_u = _even_weights(ev_w_in[li], ev_b_forget[li])
            qk, vt, z, u = _proj(x2, seq, [(w_qk, "qk", BF16), (w_vt, "nt", BF16),
                                           (w_f3, "nn", F32), (w_u, "nn", F32)])
            cum3 = _fox_cum(z.reshape(bsz, seq, LANES), b_f3)
            att = _fox_attn(qk.reshape(bsz, seq, -1), cum3, vt)
            pooled = _pool(u.reshape(bsz, seq, POOL_WIDTH), ev_pool_w[li].astype(BF16),
                           ev_pool_scale[li][None, :])
            w_out = ev_w_out[li].astype(BF16)
            x2 = _outproj_ln([att.reshape(n, FOX_WIDTH), pooled.reshape(n, POOL_WIDTH)],
                             [w_out[:FOX_WIDTH], w_out[FOX_WIDTH:]], x2, g1, b1)
            x2 = _ffn_ln(x2, ev_ffn_gate[li].astype(BF16), ev_ffn_up[li].astype(BF16),
                         ev_ffn_down[li].astype(BF16), g2, b2)
        else:
            w_qk, w_vt = _odd_weights(od_w_in[li])
            qk, vt = _proj(x2, seq, [(w_qk, "qk", BF16), (w_vt, "nt", BF16)])
            att = _moba_attn(qk.reshape(bsz, seq, -1), vt)
            x2 = _outproj_ln([att.reshape(n, MOBA_WIDTH)], [od_w_out[li].astype(BF16)], x2, g1, b1)
            x2 = _moe_ln(x2, _router_weights(od_router[li]), od_exp_gate[li].astype(BF16),
                         od_exp_up[li].astype(BF16), od_exp_down[li].astype(BF16), g2, b2)
    return x2.reshape(bsz, seq, d)
```

```python
import functools

import jax
import jax.numpy as jnp
from jax import lax
from jax.experimental import pallas as pl
from jax.experimental.pallas import tpu as pltpu

F32 = jnp.float32
BF16 = jnp.bfloat16

D_MODEL = 1024
DEPTH = 4
HEAD_DIM = 64
FOX_HEADS = 8
FOX_WIDTH = FOX_HEADS * HEAD_DIM
POOL_WINDOWS = (2, 4, 8, 16)
POOL_WIDTH = 512
POOL_GROUP_DIM = 128
MOBA_HEADS = 16
MOBA_WIDTH = MOBA_HEADS * HEAD_DIM
MOBA_BLOCK = 256
MOBA_TOPK = 3
D_FF = 3584
N_EXPERTS = 8
LN_EPS = 1e-5
ALPHA = (2 * DEPTH) ** 0.25
LOG2E = 1.4426950408889634
Q_LOG2_SCALE = HEAD_DIM ** -0.5 * LOG2E

LANES = 128
ROW_SLABS = D_MODEL // LANES
ATT_TILE = 512
ROW_TILE = 512
FFN_ROW_TILE = 1024
FFN_COL_TILE = 512
MOE_ROW_TILE = 1024
CUM_BLOCK = 256
NEG_BIG = -1e30
VMEM_LIMIT = 48 * 1024 * 1024

_NT = (((1,), (1,)), ((), ()))


def _params(sem):
    return pltpu.CompilerParams(dimension_semantics=sem, vmem_limit_bytes=VMEM_LIMIT)


def _split3(x):
    hi = x.astype(BF16)
    r = x - hi.astype(F32)
    mid = r.astype(BF16)
    lo = (r - mid.astype(F32)).astype(BF16)
    return hi, mid, lo


def _ln(y, g, b):
    mu = jnp.mean(y, axis=-1, keepdims=True)
    yc = y - mu
    var = jnp.mean(yc * yc, axis=-1, keepdims=True)
    return yc * lax.rsqrt(var + LN_EPS) * g + b


def _proj_kernel(*refs, kinds):
    x_ref = refs[0]
    w_refs = refs[1:1 + len(kinds)]
    o_refs = refs[1 + len(kinds):]
    xb = x_ref[...].astype(BF16)
    for kind, w_ref, o_ref in zip(kinds, w_refs, o_refs):
        if kind == "nt":
            m = w_ref.shape[0]
            for c0 in range(0, m, 512):
                c1 = min(c0 + 512, m)
                o_ref[0, 0, c0:c1, :] = lax.dot_general(
                    w_ref[c0:c1, :], xb, _NT, preferred_element_type=F32).astype(o_ref.dtype)
        else:
            m = w_ref.shape[1]
            for c0 in range(0, m, 512):
                c1 = min(c0 + 512, m)
                y = jnp.dot(xb, w_ref[:, c0:c1], preferred_element_type=F32)
                if kind == "qk" and c0 < m // 2:
                    y = y * Q_LOG2_SCALE
                o_ref[:, c0:c1] = y.astype(o_ref.dtype)


def _proj(x2, seq, specs):
    n, d = x2.shape
    tm = ROW_TILE
    tps = seq // tm
    kinds = tuple(k for _, k, _ in specs)
    in_specs = [pl.BlockSpec((tm, d), lambda i: (i, 0))]
    out_specs, out_shapes = [], []
    for w, kind, dt in specs:
        in_specs.append(pl.BlockSpec(w.shape, lambda i: (0, 0)))
        if kind == "nt":
            m = w.shape[0]
            out_shapes.append(jax.ShapeDtypeStruct((n // seq, tps, m, tm), dt))
            out_specs.append(pl.BlockSpec((1, 1, m, tm), lambda i: (i // tps, i % tps, 0, 0)))
        else:
            m = w.shape[1]
            out_shapes.append(jax.ShapeDtypeStruct((n, m), dt))
            out_specs.append(pl.BlockSpec((tm, m), lambda i: (i, 0)))
    return pl.pallas_call(
        functools.partial(_proj_kernel, kinds=kinds),
        grid=(n // tm,),
        in_specs=in_specs,
        out_specs=out_specs,
        out_shape=out_shapes,
        compiler_params=_params(("arbitrary",)),
        name="proj",
    )(x2, *[w for w, _, _ in specs])


def _fox_cum_kernel(z_ref, b_ref, o_ref):
    seq = z_ref.shape[1]
    blk = CUM_BLOCK
    row = lax.broadcasted_iota(jnp.int32, (blk, blk), 0)
    col = lax.broadcasted_iota(jnp.int32, (blk, blk), 1)
    tri = (col <= row).astype(BF16)
    lane = lax.broadcasted_iota(jnp.int32, (blk, LANES), 1)
    carry = jnp.zeros((1, LANES), F32)
    for t in range(seq // blk):
        logf = jax.nn.log_sigmoid(z_ref[0, t * blk:(t + 1) * blk, :] + b_ref[...])
        hi, mid, lo = _split3(logf)
        c = (jnp.dot(tri, hi, preferred_element_type=F32)
             + jnp.dot(tri, mid, preferred_element_type=F32)
             + jnp.dot(tri, lo, preferred_element_type=F32)) + carry
        carry = c[blk - 1:blk, :]
        chi, cmid, clo = _split3(c * LOG2E)
        o_ref[0, t * blk:(t + 1) * blk, :] = jnp.where(
            lane < FOX_HEADS, chi, jnp.where(lane < 2 * FOX_HEADS, cmid, clo))


def _fox_cum(z3, bias3):
    b, s, _ = z3.shape
    return pl.pallas_call(
        _fox_cum_kernel,
        grid=(b,),
        in_specs=[pl.BlockSpec((1, s, LANES), lambda i: (i, 0, 0)),
                  pl.BlockSpec((1, LANES), lambda i: (0, 0))],
        out_specs=pl.BlockSpec((1, s, LANES), lambda i: (i, 0, 0)),
        out_shape=jax.ShapeDtypeStruct((b, s, LANES), BF16),
        compiler_params=_params(("arbitrary",)),
        name="fox_cum",
    )(z3, bias3)


SUM_ROWS = 16


def _softmax_step(s, vt1, m, acc):
    m_new = jnp.maximum(m, jnp.max(s, axis=0, keepdims=True))
    p = jnp.exp2(s - m_new).astype(BF16)
    acc_new = jnp.exp2(m - m_new) * acc + jnp.dot(vt1, p, preferred_element_type=F32)
    return m_new, acc_new


def _attn_scratch(t):
    return [pltpu.VMEM((2, 2, t, t), F32), pltpu.VMEM((2, 1, t), F32),
            pltpu.VMEM((2, HEAD_DIM + SUM_ROWS, t), F32)]


def _attn_pipeline(i, ntile, scores, vt_ref, o_ref, s_ref, m_ref, acc_ref):
    tq = s_ref.shape[3]
    causal = (lax.broadcasted_iota(jnp.int32, (tq, tq), 0)
              <= lax.broadcasted_iota(jnp.int32, (tq, tq), 1))
    ones = jnp.ones((SUM_ROWS, tq), BF16)

    def produce(buf, n, diagonal=False):
        for hh, s in enumerate(scores(n)):
            s_ref[buf, hh] = jnp.where(causal, s, -jnp.inf) if diagonal else s

    def consume(buf, n):
        for hh in range(2):
            vt1 = jnp.concatenate([vt_ref[0, n, HEAD_DIM * hh:HEAD_DIM * (hh + 1), :], ones], axis=0)
            m_ref[hh], acc_ref[hh] = _softmax_step(s_ref[buf, hh], vt1, m_ref[hh], acc_ref[hh])

    m_ref[...] = jnp.full(m_ref.shape, -jnp.inf, F32)
    acc_ref[...] = jnp.zeros(acc_ref.shape, F32)
    produce(0, i, diagonal=True)
    count = i + 1

    def pair(a, c):
        produce(1, 2 * a)
        consume(0, jnp.where(a == 0, i, 2 * a - 1))
        produce(0, jnp.minimum(2 * a + 1, ntile - 1))
        consume(1, 2 * a)
        return c

    lax.fori_loop(0, count // 2, pair, 0)

    @pl.when(count % 2 == 1)
    def _():
        consume(0, jnp.where(i == 0, i, i - 1))

    o = jnp.concatenate([acc_ref[hh, 0:HEAD_DIM] / acc_ref[hh, HEAD_DIM:HEAD_DIM + 1]
                         for hh in range(2)], axis=0)
    o_ref[0] = o.T.astype(o_ref.dtype)


def _fox_attn_kernel(q_ref, k_ref, cq_ref, ck_ref, vt_ref, o_ref, kaug_ref, s_ref, m_ref, acc_ref):
    tq = q_ref.shape[1]
    ntile = k_ref.shape[1] // tq
    j = pl.program_id(1)
    i = pl.program_id(2)
    lane = lax.broadcasted_iota(jnp.int32, (1, LANES), 1)
    prow = lax.broadcasted_iota(jnp.int32, (LANES, LANES), 0)
    pcol = lax.broadcasted_iota(jnp.int32, (LANES, LANES), 1)

    def place(h, first_lane, sign):
        hit = (((prow == h) & (pcol == first_lane))
               | ((prow == h + FOX_HEADS) & (pcol == first_lane + 1))
               | ((prow == h + 2 * FOX_HEADS) & (pcol == first_lane + 2)))
        return jnp.where(hit, sign, 0.0).astype(BF16)

    def augment(x, c3, h, hh, gate_off, sign, ones_off):
        base = HEAD_DIM * (1 - hh)
        head_lanes = (lane >= HEAD_DIM * hh) & (lane < HEAD_DIM * (hh + 1))
        ones = (lane >= base + ones_off) & (lane < base + ones_off + 3)
        aug = jnp.dot(c3, place(h, base + gate_off, sign), preferred_element_type=F32)
        return jnp.where(head_lanes, x, jnp.where(ones, 1.0, aug).astype(BF16))

    @pl.when(i == 0)
    def _():
        for hh in range(2):
            for t in range(ntile):
                kaug_ref[hh, t] = augment(k_ref[0, t * tq:(t + 1) * tq, :],
                                          ck_ref[0, t * tq:(t + 1) * tq, :],
                                          2 * j + hh, hh, 0, -1.0, 3)

    qaug = [augment(q_ref[0], cq_ref[0], 2 * j + hh, hh, 3, 1.0, 0) for hh in range(2)]

    def scores(n):
        return [lax.dot_general(kaug_ref[hh, n], qaug[hh], _NT, preferred_element_type=F32)
                for hh in range(2)]

    _attn_pipeline(i, ntile, scores, vt_ref, o_ref, s_ref, m_ref, acc_ref)


def _fox_attn(qk3, cum3, vt4):
    b, s, _ = qk3.shape
    t = ATT_TILE
    npair = FOX_HEADS // 2
    return pl.pallas_call(
        _fox_attn_kernel,
        grid=(b, npair, s // t),
        in_specs=[
            pl.BlockSpec((1, t, LANES), lambda b_, j, i: (b_, i, j)),
            pl.BlockSpec((1, s, LANES), lambda b_, j, i: (b_, 0, npair + j)),
            pl.BlockSpec((1, t, LANES), lambda b_, j, i: (b_, i, 0)),
            pl.BlockSpec((1, s, LANES), lambda b_, j, i: (b_, 0, 0)),
            pl.BlockSpec((1, s // t, LANES, t), lambda b_, j, i: (b_, 0, j, 0)),
        ],
        out_specs=pl.BlockSpec((1, t, LANES), lambda b_, j, i: (b_, i, j)),
        out_shape=jax.ShapeDtypeStruct((b, s, FOX_WIDTH), BF16),
        scratch_shapes=[pltpu.VMEM((2, s // t, t, LANES), BF16)] + _attn_scratch(t),
        compiler_params=_params(("arbitrary", "arbitrary", "arbitrary")),
        name="fox_attn",
    )(qk3, qk3, cum3, cum3, vt4)


def _moba_attn_kernel(q_ref, k_ref, vt_ref, o_ref, km_ref, kaug_ref, s_ref, m_ref, acc_ref):
    tq = q_ref.shape[1]
    seq = k_ref.shape[1]
    nb = seq // MOBA_BLOCK
    nbp = km_ref.shape[1] // 3
    bpt = tq // MOBA_BLOCK
    blk_shift = MOBA_BLOCK.bit_length() - 1
    assert nbp <= HEAD_DIM
    i = pl.program_id(2)
    lane = lax.broadcasted_iota(jnp.int32, (1, LANES), 1)

    @pl.when(i == 0)
    def _():
        rows = [jnp.sum(k_ref[0, n * MOBA_BLOCK:(n + 1) * MOBA_BLOCK, :].astype(F32),
                        axis=0, keepdims=True) * (1.0 / MOBA_BLOCK) for n in range(nb)]
        if nbp > nb:
            rows.append(jnp.zeros((nbp - nb, LANES), F32))
        kmean = jnp.concatenate(rows, axis=0)
        key_blk = lax.shift_right_logical(lax.broadcasted_iota(jnp.int32, (tq, LANES), 0), blk_shift)
        lane2 = lax.broadcasted_iota(jnp.int32, (tq, LANES), 1)
        for hh in range(2):
            head_lanes = (lane >= HEAD_DIM * hh) & (lane < HEAD_DIM * (hh + 1))
            km_ref[hh] = jnp.concatenate(_split3(jnp.where(head_lanes, kmean, 0.0)), axis=0)
            for t in range(seq // tq):
                onehot = jnp.where(lane2 - HEAD_DIM * (1 - hh) == key_blk + t * bpt, 1.0, 0.0)
                kaug_ref[hh, t] = jnp.where(head_lanes, k_ref[0, t * tq:(t + 1) * tq, :],
                                            onehot.astype(BF16))

    q = q_ref[0]
    nidx = lax.broadcasted_iota(jnp.int32, (nbp, tq), 0)
    qblk = lax.shift_right_logical(i * tq + lax.broadcasted_iota(jnp.int32, (nbp, tq), 1), blk_shift)
    past = nidx < qblk
    qaug = []
    for hh in range(2):
        head_lanes = (lane >= HEAD_DIM * hh) & (lane < HEAD_DIM * (hh + 1))
        qh = jnp.where(head_lanes, q, jnp.zeros_like(q))
        g3 = lax.dot_general(km_ref[hh], qh, _NT, preferred_element_type=F32)
        gate = g3[0:nbp] + g3[nbp:2 * nbp] + g3[2 * nbp:3 * nbp]
        g = jnp.where(past, gate, -jnp.inf)
        sel = nidx == qblk
        for _ in range(MOBA_TOPK):
            first = jnp.min(jnp.where(g == jnp.max(g, axis=0, keepdims=True), nidx, nbp),
                            axis=0, keepdims=True)
            pick = nidx == first
            sel = sel | (pick & past)
            g = jnp.where(pick, -jnp.inf, g)
        base = HEAD_DIM * (1 - hh)
        rows = [jnp.zeros((base, tq), F32), jnp.where(sel, 0.0, NEG_BIG),
                jnp.zeros((LANES - base - nbp, tq), F32)]
        bias = jnp.concatenate([r for r in rows if r.shape[0]], axis=0)
        qaug.append(jnp.where(head_lanes, q, bias.T.astype(BF16)))

    def scores(n):
        return [lax.dot_general(kaug_ref[hh, n], qaug[hh], _NT, preferred_element_type=F32)
                for hh in range(2)]

    _attn_pipeline(i, seq // tq, scores, vt_ref, o_ref, s_ref, m_ref, acc_ref)


def _moba_attn(qk3, vt4):
    b, s, _ = qk3.shape
    t = ATT_TILE
    npair = MOBA_HEADS // 2
    nb = s // MOBA_BLOCK
    nbp = -(-nb // 16) * 16
    return pl.pallas_call(
        _moba_attn_kernel,
        grid=(b, npair, s // t),
        in_specs=[
            pl.BlockSpec((1, t, LANES), lambda b_, j, i: (b_, i, j)),
            pl.BlockSpec((1, s, LANES), lambda b_, j, i: (b_, 0, npair + j)),
            pl.BlockSpec((1, s // t, LANES, t), lambda b_, j, i: (b_, 0, j, 0)),
        ],
        out_specs=pl.BlockSpec((1, t, LANES), lambda b_, j, i: (b_, i, j)),
        out_shape=jax.ShapeDtypeStruct((b, s, MOBA_WIDTH), BF16),
        scratch_shapes=[pltpu.VMEM((2, 3 * nbp, LANES), BF16),
                        pltpu.VMEM((2, s // t, t, LANES), BF16)] + _attn_scratch(t),
        compiler_params=_params(("arbitrary", "arbitrary", "arbitrary")),
        name="moba_attn",
    )(qk3, qk3, vt4)


def _pool_kernel(u_ref, w_ref, sc_ref, o_ref):
    g = pl.program_id(1)
    u = u_ref[0]
    t = lax.broadcasted_iota(jnp.int32, u.shape, 0)
    wsum = u
    acc = jnp.zeros_like(u)
    for r, w in enumerate(POOL_WINDOWS):
        d = w // 2
        wsum = wsum + jnp.where(t >= d, pltpu.roll(wsum, d, 0), 0.0)
        cnt = jnp.minimum(t + 1, w).astype(F32)
        acc = jnp.where(g == r, wsum / cnt, acc)
    mixed = (acc - u).astype(BF16)
    y = jnp.dot(mixed, w_ref[0], preferred_element_type=F32) * sc_ref[...]
    o_ref[0] = y.astype(o_ref.dtype)


def _pool(u3, pool_w, pool_scale):
    b, s, _ = u3.shape
    ng = len(POOL_WINDOWS)
    return pl.pallas_call(
        _pool_kernel,
        grid=(b, ng),
        in_specs=[pl.BlockSpec((1, s, LANES), lambda b_, g: (b_, 0, g)),
                  pl.BlockSpec((1, POOL_GROUP_DIM, POOL_GROUP_DIM), lambda b_, g: (g, 0, 0)),
                  pl.BlockSpec((1, LANES), lambda b_, g: (0, g))],
        out_specs=pl.BlockSpec((1, s, LANES), lambda b_, g: (b_, 0, g)),
        out_shape=jax.ShapeDtypeStruct((b, s, POOL_WIDTH), BF16),
        compiler_params=_params(("arbitrary", "arbitrary")),
        name="pool",
    )(u3, pool_w, pool_scale)


def _outproj_ln_kernel(*refs, n_in):
    a_refs = refs[:n_in]
    w_refs = refs[n_in:2 * n_in]
    x_ref, g_ref, b_ref, o_ref = refs[2 * n_in:]
    mix = jnp.dot(a_refs[0][...], w_refs[0][...], preferred_element_type=F32)
    for a_ref, w_ref in zip(a_refs[1:], w_refs[1:]):
        mix = mix + jnp.dot(a_ref[...], w_ref[...], preferred_element_type=F32)
    o_ref[...] = _ln(ALPHA * x_ref[...] + mix, g_ref[...], b_ref[...])


def _outproj_ln(acts, weights, x2, gamma, beta):
    n, d = x2.shape
    tm = ROW_TILE
    in_specs = [pl.BlockSpec((tm, a.shape[1]), lambda i: (i, 0)) for a in acts]
    in_specs += [pl.BlockSpec(w.shape, lambda i: (0, 0)) for w in weights]
    in_specs += [pl.BlockSpec((tm, d), lambda i: (i, 0)),
                 pl.BlockSpec((1, d), lambda i: (0, 0)),
                 pl.BlockSpec((1, d), lambda i: (0, 0))]
    return pl.pallas_call(
        functools.partial(_outproj_ln_kernel, n_in=len(acts)),
        grid=(n // tm,),
        in_specs=in_specs,
        out_specs=pl.BlockSpec((tm, d), lambda i: (i, 0)),
        out_shape=jax.ShapeDtypeStruct((n, d), F32),
        compiler_params=_params(("arbitrary",)),
        name="outproj_ln",
    )(*acts, *weights, x2, gamma, beta)


def _swiglu_chunk(xb, wg_ref, wu_ref, wd_ref):
    hid = (jax.nn.silu(jnp.dot(xb, wg_ref[...].astype(BF16), preferred_element_type=F32))
           * jnp.dot(xb, wu_ref[...].astype(BF16), preferred_element_type=F32))
    return jnp.dot(hid.astype(BF16), wd_ref[...].astype(BF16), preferred_element_type=F32)


def _ffn_kernel(x_ref, wg_ref, wu_ref, wd_ref, g_ref, b_ref, o_ref, acc_ref, xb_ref):
    f = pl.program_id(1)

    @pl.when(f == 0)
    def _():
        acc_ref[...] = jnp.zeros_like(acc_ref)
        xb_ref[...] = x_ref[...].astype(BF16)

    acc_ref[...] += _swiglu_chunk(xb_ref[...], wg_ref, wu_ref, wd_ref)

    @pl.when(f == pl.num_programs(1) - 1)
    def _():
        o_ref[...] = _ln(ALPHA * x_ref[...] + acc_ref[...], g_ref[...], b_ref[...])


def _ffn_ln(x2, wg, wu, wd, gamma, beta):
    n, d = x2.shape
    tm, tf = FFN_ROW_TILE, FFN_COL_TILE
    ff = wg.shape[1]
    return pl.pallas_call(
        _ffn_kernel,
        grid=(n // tm, ff // tf),
        in_specs=[pl.BlockSpec((tm, d), lambda i, f: (i, 0)),
                  pl.BlockSpec((d, tf), lambda i, f: (0, f)),
                  pl.BlockSpec((d, tf), lambda i, f: (0, f)),
                  pl.BlockSpec((tf, d), lambda i, f: (f, 0)),
                  pl.BlockSpec((1, d), lambda i, f: (0, 0)),
                  pl.BlockSpec((1, d), lambda i, f: (0, 0))],
        out_specs=pl.BlockSpec((tm, d), lambda i, f: (i, 0)),
        out_shape=jax.ShapeDtypeStruct((n, d), F32),
        scratch_shapes=[pltpu.VMEM((tm, d), F32), pltpu.VMEM((tm, d), BF16)],
        compiler_params=_params(("arbitrary", "arbitrary")),
        name="ffn_ln",
    )(x2, wg, wu, wd, gamma, beta)


def _router_kernel(x_ref, w_ref, info_ref, cnt_ref, seen_ref):
    @pl.when(pl.program_id(0) == 0)
    def _():
        seen_ref[...] = jnp.zeros_like(seen_ref)

    x = x_ref[...]
    tm = x.shape[0]
    xh = x.astype(BF16)
    xm = (x - xh.astype(F32)).astype(BF16)
    l3 = (jnp.dot(xh, w_ref[...], preferred_element_type=F32)
          + jnp.dot(xm, w_ref[...], preferred_element_type=F32))
    logits = l3[:, 0:LANES] + l3[:, LANES:2 * LANES] + l3[:, 2 * LANES:3 * LANES]
    lane = lax.broadcasted_iota(jnp.int32, logits.shape, 1)
    g = jnp.where(lane < N_EXPERTS, logits, -jnp.inf)
    v1 = jnp.max(g, axis=1, keepdims=True)
    i1 = jnp.min(jnp.where(g == v1, lane, LANES), axis=1, keepdims=True)
    p1 = lane == i1
    g = jnp.where(p1, -jnp.inf, g)
    v2 = jnp.max(g, axis=1, keepdims=True)
    i2 = jnp.min(jnp.where(g == v2, lane, LANES), axis=1, keepdims=True)
    p2 = lane == i2
    e2 = jnp.exp(v2 - v1)
    den = 1.0 + e2
    member = jnp.where(p1 | p2, 1.0, 0.0)
    earlier = (lax.broadcasted_iota(jnp.int32, (tm, tm), 1)
               < lax.broadcasted_iota(jnp.int32, (tm, tm), 0)).astype(BF16)
    before = jnp.dot(earlier, member.astype(BF16), preferred_element_type=F32) + seen_ref[...]
    r1 = jnp.sum(jnp.where(p1, before, 0.0), axis=1, keepdims=True)
    r2 = jnp.sum(jnp.where(p2, before, 0.0), axis=1, keepdims=True)
    seen_ref[...] += jnp.sum(member, axis=0, keepdims=True)
    cnt_ref[...] = seen_ref[...]
    cols = (i1.astype(F32), i2.astype(F32), r1, r2, 1.0 / den, e2 / den)
    info = jnp.zeros_like(logits)
    for c, v in enumerate(cols):
        info = jnp.where(lane == c, v, info)
    info_ref[...] = info


def _router(x2, w3):
    n, d = x2.shape
    tm = ROW_TILE
    return pl.pallas_call(
        _router_kernel,
        grid=(n // tm,),
        in_specs=[pl.BlockSpec((tm, d), lambda i: (i, 0)),
                  pl.BlockSpec(w3.shape, lambda i: (0, 0))],
        out_specs=[pl.BlockSpec((tm, LANES), lambda i: (i, 0)),
                   pl.BlockSpec((1, LANES), lambda i: (0, 0))],
        out_shape=[jax.ShapeDtypeStruct((n, LANES), F32),
                   jax.ShapeDtypeStruct((1, LANES), F32)],
        scratch_shapes=[pltpu.VMEM((1, LANES), F32)],
        compiler_params=_params(("arbitrary",)),
        name="router",
    )(x2, w3)


def _route_plan(info, counts, n_tiles):
    tmf = MOE_ROW_TILE
    cnt = counts[0, :N_EXPERTS].astype(jnp.int32)
    padded = (cnt + tmf - 1) // tmf * tmf
    ends = jnp.cumsum(padded)
    offs = ends - padded
    eid = info[:, 0:2].astype(jnp.int32)
    rank = info[:, 2:4].astype(jnp.int32)
    pos = rank + jnp.sum(jnp.where(eid[..., None] == jnp.arange(N_EXPERTS), offs, 0), axis=-1)
    n_used = ends[-1] // tmf
    tile = jnp.arange(n_tiles)
    owner = jnp.minimum(jnp.sum(tile[:, None] * tmf >= ends[None, :], axis=1), N_EXPERTS - 1)
    owner = jnp.where(tile < n_used, owner, owner[n_used - 1])
    return (pos.reshape(-1, 1, 2 * ROW_TILE), owner.astype(jnp.int32),
            n_used.reshape(1).astype(jnp.int32))


def _to_token_major(dst_ref, val):
    tm = val.shape[0]
    for s in range(ROW_SLABS):
        dst_ref[pl.ds(s, tm, stride=ROW_SLABS), :] = val[:, s * LANES:(s + 1) * LANES]


def _from_token_major(src_ref):
    tm = src_ref.shape[0] // ROW_SLABS
    return jnp.concatenate([src_ref[pl.ds(s, tm, stride=ROW_SLABS), :] for s in range(ROW_SLABS)],
                           axis=1)


def _token_rows(ref, t):
    return ref.at[pl.ds(pl.multiple_of(t * ROW_SLABS, ROW_SLABS), ROW_SLABS), :]


def _dispatch_kernel(pos_ref, x_ref, zero_ref, xs_ref, xt_ref, sem):
    del zero_ref
    tm = x_ref.shape[0]
    _to_token_major(xt_ref, x_ref[...])

    def row_copy(t, s):
        return pltpu.make_async_copy(_token_rows(xt_ref, t),
                                     _token_rows(xs_ref, pos_ref[0, 2 * t + s]), sem)

    def issue(t, c):
        row_copy(t, 0).start()
        row_copy(t, 1).start()
        return c

    lax.fori_loop(0, tm, issue, 0, unroll=8)
    for _ in range(2):
        pltpu.make_async_copy(xt_ref, xs_ref.at[pl.ds(0, tm * ROW_SLABS), :], sem).wait()


def _dispatch(x2, pos, n_rows):
    n, d = x2.shape
    tm = ROW_TILE
    return pl.pallas_call(
        _dispatch_kernel,
        grid=(n // tm,),
        in_specs=[pl.BlockSpec((None, 1, 2 * tm), lambda i: (i, 0, 0), memory_space=pltpu.SMEM),
                  pl.BlockSpec((tm, d), lambda i: (i, 0)),
                  pl.BlockSpec(memory_space=pl.ANY)],
        out_specs=pl.BlockSpec(memory_space=pl.ANY),
        out_shape=jax.ShapeDtypeStruct((n_rows * ROW_SLABS, LANES), F32),
        input_output_aliases={2: 0},
        scratch_shapes=[pltpu.VMEM((tm * ROW_SLABS, LANES), F32), pltpu.SemaphoreType.DMA],
        compiler_params=_params(("arbitrary",)),
        name="moe_dispatch",
    )(pos, x2, jnp.zeros((n_rows * ROW_SLABS, LANES), F32))


def _moe_ffn_kernel(owner_ref, used_ref, xs_ref, wg_ref, wu_ref, wd_ref, ys_ref, acc_ref, xb_ref):
    del owner_ref
    i, f = pl.program_id(0), pl.program_id(1)
    used = i < used_ref[0]

    @pl.when(f == 0)
    def _():
        acc_ref[...] = jnp.zeros_like(acc_ref)
        xb_ref[...] = _from_token_major(xs_ref).astype(BF16)

    @pl.when(used)
    def _():
        acc_ref[...] += _swiglu_chunk(xb_ref[...], wg_ref, wu_ref, wd_ref)

    @pl.when(f == pl.num_programs(1) - 1)
    def _():
        _to_token_major(ys_ref, acc_ref[...])


def _moe_ffn(xs, owner, n_used, layer, wg, wu, wd):
    d = wg.shape[2]
    n_rows = xs.shape[0] // ROW_SLABS
    tm, tf = MOE_ROW_TILE, FFN_COL_TILE
    ff = wg.shape[3]
    nf = ff // tf
    col = lambda i, f, used: jnp.where(i < used[0], f, nf - 1)
    return pl.pallas_call(
        _moe_ffn_kernel,
        grid_spec=pltpu.PrefetchScalarGridSpec(
            num_scalar_prefetch=2,
            grid=(n_rows // tm, nf),
            in_specs=[
                pl.BlockSpec((tm * ROW_SLABS, LANES),
                             lambda i, f, own, used: (jnp.minimum(i, used[0] - 1), 0)),
                pl.BlockSpec((None, None, d, tf),
                             lambda i, f, own, used: (layer, own[i], 0, col(i, f, used))),
                pl.BlockSpec((None, None, d, tf),
                             lambda i, f, own, used: (layer, own[i], 0, col(i, f, used))),
                pl.BlockSpec((None, None, tf, d),
                             lambda i, f, own, used: (layer, own[i], col(i, f, used), 0)),
            ],
            out_specs=pl.BlockSpec((tm * ROW_SLABS, LANES), lambda i, f, own, used: (i, 0)),
            scratch_shapes=[pltpu.VMEM((tm, d), F32), pltpu.VMEM((tm, d), BF16)],
        ),
        out_shape=jax.ShapeDtypeStruct((n_rows * ROW_SLABS, LANES), F32),
        compiler_params=_params(("arbitrary", "arbitrary")),
        name="moe_ffn",
    )(owner, n_used, xs, wg, wu, wd)


def _combine_ln_kernel(pos_ref, x_ref, info_ref, ys_ref, g_ref, b_ref, o_ref, ybuf_ref, sem):
    tm = x_ref.shape[0]

    def row_copy(t, s):
        return pltpu.make_async_copy(_token_rows(ys_ref, pos_ref[0, 2 * t + s]),
                                     _token_rows(ybuf_ref.at[s], t), sem)

    def issue(t, c):
        row_copy(t, 0).start()
        row_copy(t, 1).start()
        return c

    lax.fori_loop(0, tm, issue, 0, unroll=8)
    for s in range(2):
        pltpu.make_async_copy(ys_ref.at[pl.ds(0, tm * ROW_SLABS), :], ybuf_ref.at[s], sem).wait()
    info = info_ref[...]
    lane = lax.broadcasted_iota(jnp.int32, info.shape, 1)
    w1 = jnp.sum(jnp.where(lane == 4, info, 0.0), axis=1, keepdims=True)
    w2 = jnp.sum(jnp.where(lane == 5, info, 0.0), axis=1, keepdims=True)
    y = w1 * _from_token_major(ybuf_ref.at[0]) + w2 * _from_token_major(ybuf_ref.at[1])
    o_ref[...] = _ln(ALPHA * x_ref[...] + y, g_ref[...], b_ref[...])


def _combine_ln(x2, info, pos, ys, gamma, beta):
    n, d = x2.shape
    tm = ROW_TILE
    return pl.pallas_call(
        _combine_ln_kernel,
        grid=(n // tm,),
        in_specs=[pl.BlockSpec((None, 1, 2 * tm), lambda i: (i, 0, 0), memory_space=pltpu.SMEM),
                  pl.BlockSpec((tm, d), lambda i: (i, 0)),
                  pl.BlockSpec((tm, LANES), lambda i: (i, 0)),
                  pl.BlockSpec(memory_space=pl.ANY),
                  pl.BlockSpec((1, d), lambda i: (0, 0)),
                  pl.BlockSpec((1, d), lambda i: (0, 0))],
        out_specs=pl.BlockSpec((tm, d), lambda i: (i, 0)),
        out_shape=jax.ShapeDtypeStruct((n, d), F32),
        scratch_shapes=[pltpu.VMEM((2, tm * ROW_SLABS, LANES), F32), pltpu.SemaphoreType.DMA],
        compiler_params=_params(("arbitrary",)),
        name="moe_combine_ln",
    )(pos, x2, info, ys, gamma, beta)


def _moe_ln(x2, w_router3, layer, wg, wu, wd, gamma, beta):
    n = x2.shape[0]
    n_tiles = 2 * n // MOE_ROW_TILE + N_EXPERTS
    info, counts = _router(x2, w_router3)
    pos, owner, n_used = _route_plan(info, counts, n_tiles)
    xs = _dispatch(x2, pos, n_tiles * MOE_ROW_TILE)
    ys = _moe_ffn(xs, owner, n_used, layer, wg, wu, wd)
    return _combine_ln(x2, info, pos, ys, gamma, beta)


def _pad_cols(w, width):
    return jnp.pad(w, ((0, 0), (0, width - w.shape[1])))


def _even_weights(w_in, b_forget):
    wq, wk, wv, wf, wu = jnp.split(
        w_in, [FOX_WIDTH, 2 * FOX_WIDTH, 3 * FOX_WIDTH, 3 * FOX_WIDTH + FOX_HEADS], axis=1)
    w_qk = jnp.concatenate([wq, wk], axis=1).astype(BF16)
    w_vt = wv.T.astype(BF16)
    w_f3 = _pad_cols(jnp.concatenate([wf, wf, wf], axis=1), LANES).astype(BF16)
    b_f3 = _pad_cols(jnp.concatenate([b_forget, b_forget, b_forget])[None, :], LANES)
    return w_qk, w_vt, w_f3, b_f3, wu.astype(BF16)


def _odd_weights(w_in):
    wq, wk, wv = jnp.split(w_in, 3, axis=1)
    return jnp.concatenate([wq, wk], axis=1).astype(BF16), wv.T.astype(BF16)


def _router_weights(w_router):
    return jnp.concatenate([_pad_cols(p.astype(F32), LANES).astype(BF16)
                            for p in _split3(w_router)], axis=1)


def kernel(x, ln1_g, ln1_b, ln2_g, ln2_b, ev_w_in, ev_b_forget, ev_pool_w, ev_pool_scale, ev_w_out,
           ev_ffn_gate, ev_ffn_up, ev_ffn_down, od_w_in, od_w_out, od_router, od_exp_gate, od_exp_up,
           od_exp_down):
    bsz, seq, d = x.shape
    n = bsz * seq
    x2 = x.reshape(n, d)
    for layer in range(DEPTH):
        li = layer // 2
        g1, b1 = ln1_g[layer][None, :], ln1_b[layer][None, :]
        g2, b2 = ln2_g[layer][None, :], ln2_b[layer][None, :]
        if layer % 2 == 0:
            w_qk, w_vt, w_f3, b_f3, w_u = _even_weights(ev_w_in[li], ev_b_forget[li])
            qk, vt, z, u = _proj(x2, seq, [(w_qk, "qk", BF16), (w_vt, "nt", BF16),
                                           (w_f3, "nn", F32), (w_u, "nn", F32)])
            cum3 = _fox_cum(z.reshape(bsz, seq, LANES), b_f3)
            att = _fox_attn(qk.reshape(bsz, seq, -1), cum3, vt)
            pooled = _pool(u.reshape(bsz, seq, POOL_WIDTH), ev_pool_w[li].astype(BF16),
                           ev_pool_scale[li][None, :])
            w_out = ev_w_out[li].astype(BF16)
            x2 = _outproj_ln([att.reshape(n, FOX_WIDTH), pooled.reshape(n, POOL_WIDTH)],
                             [w_out[:FOX_WIDTH], w_out[FOX_WIDTH:]], x2, g1, b1)
            x2 = _ffn_ln(x2, ev_ffn_gate[li].astype(BF16), ev_ffn_up[li].astype(BF16),
                         ev_ffn_down[li].astype(BF16), g2, b2)
        else:
            w_qk, w_vt = _odd_weights(od_w_in[li])
            qk, vt = _proj(x2, seq, [(w_qk, "qk", BF16), (w_vt, "nt", BF16)])
            att = _moba_attn(qk.reshape(bsz, seq, -1), vt)
            x2 = _outproj_ln([att.reshape(n, MOBA_WIDTH)], [od_w_out[li].astype(BF16)], x2, g1, b1)
            x2 = _moe_ln(x2, _router_weights(od_router[li]), li, od_exp_gate, od_exp_up, od_exp_down,
                         g2, b2)
    return x2.reshape(bsz, seq, d)
```

```python
import functools

import jax
import jax.numpy as jnp
from jax import lax
from jax.experimental import pallas as pl
from jax.experimental.pallas import tpu as pltpu

F32 = jnp.float32
BF16 = jnp.bfloat16

D_MODEL = 1024
DEPTH = 4
HEAD_DIM = 64
FOX_HEADS = 8
FOX_WIDTH = FOX_HEADS * HEAD_DIM
POOL_WINDOWS = (2, 4, 8, 16)
POOL_WIDTH = 512
POOL_GROUP_DIM = 128
MOBA_HEADS = 16
MOBA_WIDTH = MOBA_HEADS * HEAD_DIM
MOBA_BLOCK = 256
MOBA_TOPK = 3
D_FF = 3584
N_EXPERTS = 8
LN_EPS = 1e-5
ALPHA = (2 * DEPTH) ** 0.25
LOG2E = 1.4426950408889634
Q_LOG2_SCALE = HEAD_DIM ** -0.5 * LOG2E

LANES = 128
ROW_SLABS = D_MODEL // LANES
ATT_TILE = 512
ROW_TILE = 512
FFN_ROW_TILE = 1024
FFN_COL_TILE = 512
MOE_ROW_TILE = 1024
CUM_BLOCK = 256
NEG_BIG = -1e30
VMEM_LIMIT = 48 * 1024 * 1024

_NT = (((1,), (1,)), ((), ()))


def _params(sem):
    return pltpu.CompilerParams(dimension_semantics=sem, vmem_limit_bytes=VMEM_LIMIT)


def _split3(x):
    hi = x.astype(BF16)
    r = x - hi.astype(F32)
    mid = r.astype(BF16)
    lo = (r - mid.astype(F32)).astype(BF16)
    return hi, mid, lo


def _ln(y, g, b):
    mu = jnp.mean(y, axis=-1, keepdims=True)
    yc = y - mu
    var = jnp.mean(yc * yc, axis=-1, keepdims=True)
    return yc * lax.rsqrt(var + LN_EPS) * g + b


def _proj_kernel(*refs, kinds):
    x_ref = refs[0]
    w_refs = refs[1:1 + len(kinds)]
    o_refs = refs[1 + len(kinds):]
    xb = x_ref[...].astype(BF16)
    for kind, w_ref, o_ref in zip(kinds, w_refs, o_refs):
        if kind == "nt":
            m = w_ref.shape[0]
            for c0 in range(0, m, 512):
                c1 = min(c0 + 512, m)
                o_ref[0, 0, c0:c1, :] = lax.dot_general(
                    w_ref[c0:c1, :], xb, _NT, preferred_element_type=F32).astype(o_ref.dtype)
        else:
            m = w_ref.shape[1]
            for c0 in range(0, m, 512):
                c1 = min(c0 + 512, m)
                y = jnp.dot(xb, w_ref[:, c0:c1], preferred_element_type=F32)
                if kind == "qk" and c0 < m // 2:
                    y = y * Q_LOG2_SCALE
                o_ref[:, c0:c1] = y.astype(o_ref.dtype)


def _proj(x2, seq, specs):
    n, d = x2.shape
    tm = ROW_TILE
    tps = seq // tm
    kinds = tuple(k for _, k, _ in specs)
    in_specs = [pl.BlockSpec((tm, d), lambda i: (i, 0))]
    out_specs, out_shapes = [], []
    for w, kind, dt in specs:
        in_specs.append(pl.BlockSpec(w.shape, lambda i: (0, 0)))
        if kind == "nt":
            m = w.shape[0]
            out_shapes.append(jax.ShapeDtypeStruct((n // seq, tps, m, tm), dt))
            out_specs.append(pl.BlockSpec((1, 1, m, tm), lambda i: (i // tps, i % tps, 0, 0)))
        else:
            m = w.shape[1]
            out_shapes.append(jax.ShapeDtypeStruct((n, m), dt))
            out_specs.append(pl.BlockSpec((tm, m), lambda i: (i, 0)))
    return pl.pallas_call(
        functools.partial(_proj_kernel, kinds=kinds),
        grid=(n // tm,),
        in_specs=in_specs,
        out_specs=out_specs,
        out_shape=out_shapes,
        compiler_params=_params(("arbitrary",)),
        name="proj",
    )(x2, *[w for w, _, _ in specs])


def _fox_cum_kernel(z_ref, b_ref, o_ref):
    seq = z_ref.shape[1]
    blk = CUM_BLOCK
    row = lax.broadcasted_iota(jnp.int32, (blk, blk), 0)
    col = lax.broadcasted_iota(jnp.int32, (blk, blk), 1)
    tri = (col <= row).astype(BF16)
    lane = lax.broadcasted_iota(jnp.int32, (blk, LANES), 1)
    carry = jnp.zeros((1, LANES), F32)
    for t in range(seq // blk):
        logf = jax.nn.log_sigmoid(z_ref[0, t * blk:(t + 1) * blk, :] + b_ref[...])
        hi, mid, lo = _split3(logf)
        c = (jnp.dot(tri, hi, preferred_element_type=F32)
             + jnp.dot(tri, mid, preferred_element_type=F32)
             + jnp.dot(tri, lo, preferred_element_type=F32)) + carry
        carry = c[blk - 1:blk, :]
        chi, cmid, clo = _split3(c * LOG2E)
        o_ref[0, t * blk:(t + 1) * blk, :] = jnp.where(
            lane < FOX_HEADS, chi, jnp.where(lane < 2 * FOX_HEADS, cmid, clo))


def _fox_cum(z3, bias3):
    b, s, _ = z3.shape
    return pl.pallas_call(
        _fox_cum_kernel,
        grid=(b,),
        in_specs=[pl.BlockSpec((1, s, LANES), lambda i: (i, 0, 0)),
                  pl.BlockSpec((1, LANES), lambda i: (0, 0))],
        out_specs=pl.BlockSpec((1, s, LANES), lambda i: (i, 0, 0)),
        out_shape=jax.ShapeDtypeStruct((b, s, LANES), BF16),
        compiler_params=_params(("arbitrary",)),
        name="fox_cum",
    )(z3, bias3)


SUM_ROWS = 16


def _softmax_step(s, vt1, m, acc):
    m_new = jnp.maximum(m, jnp.max(s, axis=0, keepdims=True))
    p = jnp.exp2(s - m_new).astype(BF16)
    acc_new = jnp.exp2(m - m_new) * acc + jnp.dot(vt1, p, preferred_element_type=F32)
    return m_new, acc_new


def _attn_scratch(s, t):
    nt = s // t
    return [pltpu.VMEM((2, nt, t, LANES), BF16), pltpu.VMEM((2, nt, t, LANES), BF16),
            pltpu.VMEM((2, 2, t, t), F32), pltpu.VMEM((nt, 2, 1, t), F32),
            pltpu.VMEM((nt, 2, HEAD_DIM + SUM_ROWS, t), F32)]


def _visit_tables(ntile):
    pairs = [(q, q - d) for d in range(1, ntile) for q in range(d, ntile)]
    if not pairs:
        pairs = [(0, 0)]
    return (jnp.asarray([q for q, _ in pairs], jnp.int32), jnp.asarray([k for _, k in pairs], jnp.int32))


def _attn_sweep(vq_ref, vk_ref, vt_ref, o_ref, kaug_ref, qaug_ref, s_ref, m_ref, acc_ref):
    ntile, tq = qaug_ref.shape[1], qaug_ref.shape[2]
    nvis = ntile * (ntile - 1) // 2
    causal = (lax.broadcasted_iota(jnp.int32, (tq, tq), 0)
              <= lax.broadcasted_iota(jnp.int32, (tq, tq), 1))
    ones = jnp.ones((SUM_ROWS, tq), BF16)

    def produce(buf, qi, ki, diagonal=False):
        for hh in range(2):
            s = lax.dot_general(kaug_ref[hh, ki], qaug_ref[hh, qi], _NT, preferred_element_type=F32)
            s_ref[buf, hh] = jnp.where(causal, s, -jnp.inf) if diagonal else s

    def consume(buf, qi, ki):
        for hh in range(2):
            vt1 = jnp.concatenate([vt_ref[0, ki, HEAD_DIM * hh:HEAD_DIM * (hh + 1), :], ones], axis=0)
            m_ref[qi, hh], acc_ref[qi, hh] = _softmax_step(
                s_ref[buf, hh], vt1, m_ref[qi, hh], acc_ref[qi, hh])

    def pipeline(count, visit, diagonal):
        last = count - 1
        produce(0, *visit(0), diagonal)

        def pair(a, c):
            produce(1, *visit(2 * a + 1), diagonal)
            consume(0, *visit(2 * a))
            produce(0, *visit(jnp.minimum(2 * a + 2, last)), diagonal)
            consume(1, *visit(2 * a + 1))
            return c

        lax.fori_loop(0, count // 2, pair, 0)
        if count % 2:
            consume(0, *visit(last))

    m_ref[...] = jnp.full(m_ref.shape, -jnp.inf, F32)
    acc_ref[...] = jnp.zeros(acc_ref.shape, F32)
    pipeline(ntile, lambda v: (v, v), True)
    if nvis:
        pipeline(nvis, lambda v: (vq_ref[v], vk_ref[v]), False)
    for qi in range(ntile):
        o = jnp.concatenate([acc_ref[qi, hh, 0:HEAD_DIM] / acc_ref[qi, hh, HEAD_DIM:HEAD_DIM + 1]
                             for hh in range(2)], axis=0)
        o_ref[0, qi * tq:(qi + 1) * tq, :] = o.T.astype(o_ref.dtype)


def _attn_call(kernel_fn, name, b, s, npair, width, operands, in_specs, extra_scratch=()):
    t = ATT_TILE
    vq, vk = _visit_tables(s // t)
    smem = pl.BlockSpec(memory_space=pltpu.SMEM)
    return pl.pallas_call(
        kernel_fn,
        grid=(b, npair),
        in_specs=[smem, smem] + in_specs,
        out_specs=pl.BlockSpec((1, s, LANES), lambda b_, j: (b_, 0, j)),
        out_shape=jax.ShapeDtypeStruct((b, s, width), BF16),
        scratch_shapes=list(extra_scratch) + _attn_scratch(s, t),
        compiler_params=_params(("arbitrary", "arbitrary")),
        name=name,
    )(vq, vk, *operands)


def _fox_attn_kernel(vq_ref, vk_ref, q_ref, k_ref, c_ref, vt_ref, o_ref,
                     kaug_ref, qaug_ref, s_ref, m_ref, acc_ref):
    ntile, tq = qaug_ref.shape[1], qaug_ref.shape[2]
    j = pl.program_id(1)
    lane = lax.broadcasted_iota(jnp.int32, (1, LANES), 1)
    prow = lax.broadcasted_iota(jnp.int32, (LANES, LANES), 0)
    pcol = lax.broadcasted_iota(jnp.int32, (LANES, LANES), 1)

    def place(h, first_lane, sign):
        hit = (((prow == h) & (pcol == first_lane))
               | ((prow == h + FOX_HEADS) & (pcol == first_lane + 1))
               | ((prow == h + 2 * FOX_HEADS) & (pcol == first_lane + 2)))
        return jnp.where(hit, sign, 0.0).astype(BF16)

    def augment(x, c3, h, hh, gate_off, sign, ones_off):
        base = HEAD_DIM * (1 - hh)
        head_lanes = (lane >= HEAD_DIM * hh) & (lane < HEAD_DIM * (hh + 1))
        ones = (lane >= base + ones_off) & (lane < base + ones_off + 3)
        aug = jnp.dot(c3, place(h, base + gate_off, sign), preferred_element_type=F32)
        return jnp.where(head_lanes, x, jnp.where(ones, 1.0, aug).astype(BF16))

    for hh in range(2):
        for t in range(ntile):
            rows = slice(t * tq, (t + 1) * tq)
            kaug_ref[hh, t] = augment(k_ref[0, rows, :], c_ref[0, rows, :], 2 * j + hh, hh, 0, -1.0, 3)
            qaug_ref[hh, t] = augment(q_ref[0, rows, :], c_ref[0, rows, :], 2 * j + hh, hh, 3, 1.0, 0)

    _attn_sweep(vq_ref, vk_ref, vt_ref, o_ref, kaug_ref, qaug_ref, s_ref, m_ref, acc_ref)


def _fox_attn(qk3, cum3, vt4):
    b, s, _ = qk3.shape
    t = ATT_TILE
    npair = FOX_HEADS // 2
    in_specs = [pl.BlockSpec((1, s, LANES), lambda b_, j: (b_, 0, j)),
                pl.BlockSpec((1, s, LANES), lambda b_, j: (b_, 0, npair + j)),
                pl.BlockSpec((1, s, LANES), lambda b_, j: (b_, 0, 0)),
                pl.BlockSpec((1, s // t, LANES, t), lambda b_, j: (b_, 0, j, 0))]
    return _attn_call(_fox_attn_kernel, "fox_attn", b, s, npair, FOX_WIDTH,
                      (qk3, qk3, cum3, vt4), in_specs)


def _moba_attn_kernel(vq_ref, vk_ref, q_ref, k_ref, vt_ref, o_ref,
                      km_ref, kaug_ref, qaug_ref, s_ref, m_ref, acc_ref):
    ntile, tq = qaug_ref.shape[1], qaug_ref.shape[2]
    seq = k_ref.shape[1]
    nb = seq // MOBA_BLOCK
    nbp = km_ref.shape[1] // 3
    bpt = tq // MOBA_BLOCK
    blk_shift = MOBA_BLOCK.bit_length() - 1
    assert nbp <= HEAD_DIM
    lane = lax.broadcasted_iota(jnp.int32, (1, LANES), 1)

    rows = [jnp.sum(k_ref[0, n * MOBA_BLOCK:(n + 1) * MOBA_BLOCK, :].astype(F32),
                    axis=0, keepdims=True) * (1.0 / MOBA_BLOCK) for n in range(nb)]
    if nbp > nb:
        rows.append(jnp.zeros((nbp - nb, LANES), F32))
    kmean = jnp.concatenate(rows, axis=0)
    key_blk = lax.shift_right_logical(lax.broadcasted_iota(jnp.int32, (tq, LANES), 0), blk_shift)
    lane2 = lax.broadcasted_iota(jnp.int32, (tq, LANES), 1)
    nidx = lax.broadcasted_iota(jnp.int32, (nbp, tq), 0)
    qcol = lax.broadcasted_iota(jnp.int32, (nbp, tq), 1)
    for hh in range(2):
        head_lanes = (lane >= HEAD_DIM * hh) & (lane < HEAD_DIM * (hh + 1))
        base = HEAD_DIM * (1 - hh)
        km_ref[hh] = jnp.concatenate(_split3(jnp.where(head_lanes, kmean, 0.0)), axis=0)
        for t in range(ntile):
            onehot = jnp.where(lane2 - base == key_blk + t * bpt, 1.0, 0.0)
            kaug_ref[hh, t] = jnp.where(head_lanes, k_ref[0, t * tq:(t + 1) * tq, :],
                                        onehot.astype(BF16))
        for t in range(ntile):
            q = q_ref[0, t * tq:(t + 1) * tq, :]
            qblk = lax.shift_right_logical(t * tq + qcol, blk_shift)
            past = nidx < qblk
            qh = jnp.where(head_lanes, q, jnp.zeros_like(q))
            g3 = lax.dot_general(km_ref[hh], qh, _NT, preferred_element_type=F32)
            gate = g3[0:nbp] + g3[nbp:2 * nbp] + g3[2 * nbp:3 * nbp]
            g = jnp.where(past, gate, -jnp.inf)
            sel = nidx == qblk
            for _ in range(MOBA_TOPK):
                first = jnp.min(jnp.where(g == jnp.max(g, axis=0, keepdims=True), nidx, nbp),
                                axis=0, keepdims=True)
                pick = nidx == first
                sel = sel | (pick & past)
                g = jnp.where(pick, -jnp.inf, g)
            parts = [jnp.zeros((base, tq), F32), jnp.where(sel, 0.0, NEG_BIG),
                     jnp.zeros((LANES - base - nbp, tq), F32)]
            bias = jnp.concatenate([r for r in parts if r.shape[0]], axis=0)
            qaug_ref[hh, t] = jnp.where(head_lanes, q, bias.T.astype(BF16))

    _attn_sweep(vq_ref, vk_ref, vt_ref, o_ref, kaug_ref, qaug_ref, s_ref, m_ref, acc_ref)


def _moba_attn(qk3, vt4):
    b, s, _ = qk3.shape
    t = ATT_TILE
    npair = MOBA_HEADS // 2
    nbp = -(-(s // MOBA_BLOCK) // 16) * 16
    in_specs = [pl.BlockSpec((1, s, LANES), lambda b_, j: (b_, 0, j)),
                pl.BlockSpec((1, s, LANES), lambda b_, j: (b_, 0, npair + j)),
                pl.BlockSpec((1, s // t, LANES, t), lambda b_, j: (b_, 0, j, 0))]
    return _attn_call(_moba_attn_kernel, "moba_attn", b, s, npair, MOBA_WIDTH,
                      (qk3, qk3, vt4), in_specs,
                      extra_scratch=[pltpu.VMEM((2, 3 * nbp, LANES), BF16)])


def _pool_kernel(u_ref, w_ref, sc_ref, o_ref):
    g = pl.program_id(1)
    u = u_ref[0]
    t = lax.broadcasted_iota(jnp.int32, u.shape, 0)
    wsum = u
    acc = jnp.zeros_like(u)
    for r, w in enumerate(POOL_WINDOWS):
        d = w // 2
        wsum = wsum + jnp.where(t >= d, pltpu.roll(wsum, d, 0), 0.0)
        cnt = jnp.minimum(t + 1, w).astype(F32)
        acc = jnp.where(g == r, wsum / cnt, acc)
    mixed = (acc - u).astype(BF16)
    y = jnp.dot(mixed, w_ref[0], preferred_element_type=F32) * sc_ref[...]
    o_ref[0] = y.astype(o_ref.dtype)


def _pool(u3, pool_w, pool_scale):
    b, s, _ = u3.shape
    ng = len(POOL_WINDOWS)
    return pl.pallas_call(
        _pool_kernel,
        grid=(b, ng),
        in_specs=[pl.BlockSpec((1, s, LANES), lambda b_, g: (b_, 0, g)),
                  pl.BlockSpec((1, POOL_GROUP_DIM, POOL_GROUP_DIM), lambda b_, g: (g, 0, 0)),
                  pl.BlockSpec((1, LANES), lambda b_, g: (0, g))],
        out_specs=pl.BlockSpec((1, s, LANES), lambda b_, g: (b_, 0, g)),
        out_shape=jax.ShapeDtypeStruct((b, s, POOL_WIDTH), BF16),
        compiler_params=_params(("arbitrary", "arbitrary")),
        name="pool",
    )(u3, pool_w, pool_scale)


def _outproj_ln_kernel(*refs, n_in):
    a_refs = refs[:n_in]
    w_refs = refs[n_in:2 * n_in]
    x_ref, g_ref, b_ref, o_ref = refs[2 * n_in:]
    mix = jnp.dot(a_refs[0][...], w_refs[0][...], preferred_element_type=F32)
    for a_ref, w_ref in zip(a_refs[1:], w_refs[1:]):
        mix = mix + jnp.dot(a_ref[...], w_ref[...], preferred_element_type=F32)
    o_ref[...] = _ln(ALPHA * x_ref[...] + mix, g_ref[...], b_ref[...])


def _outproj_ln(acts, weights, x2, gamma, beta):
    n, d = x2.shape
    tm = ROW_TILE
    in_specs = [pl.BlockSpec((tm, a.shape[1]), lambda i: (i, 0)) for a in acts]
    in_specs += [pl.BlockSpec(w.shape, lambda i: (0, 0)) for w in weights]
    in_specs += [pl.BlockSpec((tm, d), lambda i: (i, 0)),
                 pl.BlockSpec((1, d), lambda i: (0, 0)),
                 pl.BlockSpec((1, d), lambda i: (0, 0))]
    return pl.pallas_call(
        functools.partial(_outproj_ln_kernel, n_in=len(acts)),
        grid=(n // tm,),
        in_specs=in_specs,
        out_specs=pl.BlockSpec((tm, d), lambda i: (i, 0)),
        out_shape=jax.ShapeDtypeStruct((n, d), F32),
        compiler_params=_params(("arbitrary",)),
        name="outproj_ln",
    )(*acts, *weights, x2, gamma, beta)


def _swiglu_chunk(xb, wg_ref, wu_ref, wd_ref):
    hid = (jax.nn.silu(jnp.dot(xb, wg_ref[...].astype(BF16), preferred_element_type=F32))
           * jnp.dot(xb, wu_ref[...].astype(BF16), preferred_element_type=F32))
    return jnp.dot(hid.astype(BF16), wd_ref[...].astype(BF16), preferred_element_type=F32)


def _ffn_kernel(x_ref, wg_ref, wu_ref, wd_ref, g_ref, b_ref, o_ref, acc_ref, xb_ref):
    f = pl.program_id(1)

    @pl.when(f == 0)
    def _():
        acc_ref[...] = jnp.zeros_like(acc_ref)
        xb_ref[...] = x_ref[...].astype(BF16)

    acc_ref[...] += _swiglu_chunk(xb_ref[...], wg_ref, wu_ref, wd_ref)

    @pl.when(f == pl.num_programs(1) - 1)
    def _():
        o_ref[...] = _ln(ALPHA * x_ref[...] + acc_ref[...], g_ref[...], b_ref[...])


def _ffn_ln(x2, wg, wu, wd, gamma, beta):
    n, d = x2.shape
    tm, tf = FFN_ROW_TILE, FFN_COL_TILE
    ff = wg.shape[1]
    return pl.pallas_call(
        _ffn_kernel,
        grid=(n // tm, ff // tf),
        in_specs=[pl.BlockSpec((tm, d), lambda i, f: (i, 0)),
                  pl.BlockSpec((d, tf), lambda i, f: (0, f)),
                  pl.BlockSpec((d, tf), lambda i, f: (0, f)),
                  pl.BlockSpec((tf, d), lambda i, f: (f, 0)),
                  pl.BlockSpec((1, d), lambda i, f: (0, 0)),
                  pl.BlockSpec((1, d), lambda i, f: (0, 0))],
        out_specs=pl.BlockSpec((tm, d), lambda i, f: (i, 0)),
        out_shape=jax.ShapeDtypeStruct((n, d), F32),
        scratch_shapes=[pltpu.VMEM((tm, d), F32), pltpu.VMEM((tm, d), BF16)],
        compiler_params=_params(("arbitrary", "arbitrary")),
        name="ffn_ln",
    )(x2, wg, wu, wd, gamma, beta)


def _router_kernel(x_ref, w_ref, info_ref, cnt_ref, seen_ref):
    @pl.when(pl.program_id(0) == 0)
    def _():
        seen_ref[...] = jnp.zeros_like(seen_ref)

    x = x_ref[...]
    tm = x.shape[0]
    xh = x.astype(BF16)
    xm = (x - xh.astype(F32)).astype(BF16)
    l3 = (jnp.dot(xh, w_ref[...], preferred_element_type=F32)
          + jnp.dot(xm, w_ref[...], preferred_element_type=F32))
    logits = l3[:, 0:LANES] + l3[:, LANES:2 * LANES] + l3[:, 2 * LANES:3 * LANES]
    lane = lax.broadcasted_iota(jnp.int32, logits.shape, 1)
    g = jnp.where(lane < N_EXPERTS, logits, -jnp.inf)
    v1 = jnp.max(g, axis=1, keepdims=True)
    i1 = jnp.min(jnp.where(g == v1, lane, LANES), axis=1, keepdims=True)
    p1 = lane == i1
    g = jnp.where(p1, -jnp.inf, g)
    v2 = jnp.max(g, axis=1, keepdims=True)
    i2 = jnp.min(jnp.where(g == v2, lane, LANES), axis=1, keepdims=True)
    p2 = lane == i2
    e2 = jnp.exp(v2 - v1)
    den = 1.0 + e2
    member = jnp.where(p1 | p2, 1.0, 0.0)
    earlier = (lax.broadcasted_iota(jnp.int32, (tm, tm), 1)
               < lax.broadcasted_iota(jnp.int32, (tm, tm), 0)).astype(BF16)
    before = jnp.dot(earlier, member.astype(BF16), preferred_element_type=F32) + seen_ref[...]
    r1 = jnp.sum(jnp.where(p1, before, 0.0), axis=1, keepdims=True)
    r2 = jnp.sum(jnp.where(p2, before, 0.0), axis=1, keepdims=True)
    seen_ref[...] += jnp.sum(member, axis=0, keepdims=True)
    cnt_ref[...] = seen_ref[...]
    cols = (i1.astype(F32), i2.astype(F32), r1, r2, 1.0 / den, e2 / den)
    info = jnp.zeros_like(logits)
    for c, v in enumerate(cols):
        info = jnp.where(lane == c, v, info)
    info_ref[...] = info


def _router(x2, w3):
    n, d = x2.shape
    tm = ROW_TILE
    return pl.pallas_call(
        _router_kernel,
        grid=(n // tm,),
        in_specs=[pl.BlockSpec((tm, d), lambda i: (i, 0)),
                  pl.BlockSpec(w3.shape, lambda i: (0, 0))],
        out_specs=[pl.BlockSpec((tm, LANES), lambda i: (i, 0)),
                   pl.BlockSpec((1, LANES), lambda i: (0, 0))],
        out_shape=[jax.ShapeDtypeStruct((n, LANES), F32),
                   jax.ShapeDtypeStruct((1, LANES), F32)],
        scratch_shapes=[pltpu.VMEM((1, LANES), F32)],
        compiler_params=_params(("arbitrary",)),
        name="router",
    )(x2, w3)


def _route_plan(info, counts, n_tiles):
    tmf = MOE_ROW_TILE
    cnt = counts[0, :N_EXPERTS].astype(jnp.int32)
    padded = (cnt + tmf - 1) // tmf * tmf
    ends = jnp.cumsum(padded)
    offs = ends - padded
    eid = info[:, 0:2].astype(jnp.int32)
    rank = info[:, 2:4].astype(jnp.int32)
    pos = rank + jnp.sum(jnp.where(eid[..., None] == jnp.arange(N_EXPERTS), offs, 0), axis=-1)
    n_used = ends[-1] // tmf
    tile = jnp.arange(n_tiles)
    owner = jnp.minimum(jnp.sum(tile[:, None] * tmf >= ends[None, :], axis=1), N_EXPERTS - 1)
    owner = jnp.where(tile < n_used, owner, owner[n_used - 1])
    return (pos.reshape(-1, 1, 2 * ROW_TILE), owner.astype(jnp.int32),
            n_used.reshape(1).astype(jnp.int32))


def _to_token_major(dst_ref, val):
    tm = val.shape[0]
    for s in range(ROW_SLABS):
        dst_ref[pl.ds(s, tm, stride=ROW_SLABS), :] = val[:, s * LANES:(s + 1) * LANES]


def _from_token_major(src_ref):
    tm = src_ref.shape[0] // ROW_SLABS
    return jnp.concatenate([src_ref[pl.ds(s, tm, stride=ROW_SLABS), :] for s in range(ROW_SLABS)],
                           axis=1)


def _token_rows(ref, t):
    return ref.at[pl.ds(pl.multiple_of(t * ROW_SLABS, ROW_SLABS), ROW_SLABS), :]


def _dispatch_kernel(pos_ref, x_ref, zero_ref, xs_ref, xt_ref, sem):
    del zero_ref
    tm = x_ref.shape[0]
    _to_token_major(xt_ref, x_ref[...])

    def row_copy(t, s):
        return pltpu.make_async_copy(_token_rows(xt_ref, t),
                                     _token_rows(xs_ref, pos_ref[0, 2 * t + s]), sem)

    def issue(t, c):
        row_copy(t, 0).start()
        row_copy(t, 1).start(priority=1)
        return c

    lax.fori_loop(0, tm, issue, 0, unroll=8)
    for _ in range(2):
        pltpu.make_async_copy(xt_ref, xs_ref.at[pl.ds(0, tm * ROW_SLABS), :], sem).wait()


def _dispatch(x2, pos, n_rows):
    n, d = x2.shape
    tm = ROW_TILE
    return pl.pallas_call(
        _dispatch_kernel,
        grid=(n // tm,),
        in_specs=[pl.BlockSpec((None, 1, 2 * tm), lambda i: (i, 0, 0), memory_space=pltpu.SMEM),
                  pl.BlockSpec((tm, d), lambda i: (i, 0)),
                  pl.BlockSpec(memory_space=pl.ANY)],
        out_specs=pl.BlockSpec(memory_space=pl.ANY),
        out_shape=jax.ShapeDtypeStruct((n_rows * ROW_SLABS, LANES), F32),
        input_output_aliases={2: 0},
        scratch_shapes=[pltpu.VMEM((tm * ROW_SLABS, LANES), F32), pltpu.SemaphoreType.DMA],
        compiler_params=_params(("arbitrary",)),
        name="moe_dispatch",
    )(pos, x2, jnp.zeros((n_rows * ROW_SLABS, LANES), F32))


def _moe_ffn_kernel(owner_ref, used_ref, xs_ref, wg_ref, wu_ref, wd_ref, ys_ref, acc_ref, xb_ref):
    del owner_ref
    i, f = pl.program_id(0), pl.program_id(1)
    used = i < used_ref[0]

    @pl.when(f == 0)
    def _():
        acc_ref[...] = jnp.zeros_like(acc_ref)
        xb_ref[...] = _from_token_major(xs_ref).astype(BF16)

    @pl.when(used)
    def _():
        acc_ref[...] += _swiglu_chunk(xb_ref[...], wg_ref, wu_ref, wd_ref)

    @pl.when(f == pl.num_programs(1) - 1)
    def _():
        _to_token_major(ys_ref, acc_ref[...])


def _moe_ffn(xs, owner, n_used, layer, wg, wu, wd):
    d = wg.shape[2]
    n_rows = xs.shape[0] // ROW_SLABS
    tm, tf = MOE_ROW_TILE, FFN_COL_TILE
    ff = wg.shape[3]
    nf = ff // tf
    col = lambda i, f, used: jnp.where(i < used[0], f, nf - 1)
    return pl.pallas_call(
        _moe_ffn_kernel,
        grid_spec=pltpu.PrefetchScalarGridSpec(
            num_scalar_prefetch=2,
            grid=(n_rows // tm, nf),
            in_specs=[
                pl.BlockSpec((tm * ROW_SLABS, LANES),
                             lambda i, f, own, used: (jnp.minimum(i, used[0] - 1), 0)),
                pl.BlockSpec((None, None, d, tf),
                             lambda i, f, own, used: (layer, own[i], 0, col(i, f, used))),
                pl.BlockSpec((None, None, d, tf),
                             lambda i, f, own, used: (layer, own[i], 0, col(i, f, used))),
                pl.BlockSpec((None, None, tf, d),
                             lambda i, f, own, used: (layer, own[i], col(i, f, used), 0)),
            ],
            out_specs=pl.BlockSpec((tm * ROW_SLABS, LANES), lambda i, f, own, used: (i, 0)),
            scratch_shapes=[pltpu.VMEM((tm, d), F32), pltpu.VMEM((tm, d), BF16)],
        ),
        out_shape=jax.ShapeDtypeStruct((n_rows * ROW_SLABS, LANES), F32),
        compiler_params=_params(("arbitrary", "arbitrary")),
        name="moe_ffn",
    )(owner, n_used, xs, wg, wu, wd)


def _combine_ln_kernel(pos_ref, x_ref, info_ref, ys_ref, g_ref, b_ref, o_ref, ybuf_ref, sem):
    tm = x_ref.shape[0]

    def row_copy(t, s):
        return pltpu.make_async_copy(_token_rows(ys_ref, pos_ref[0, 2 * t + s]),
                                     _token_rows(ybuf_ref.at[s], t), sem)

    def issue(t, c):
        row_copy(t, 0).start()
        row_copy(t, 1).start(priority=1)
        return c

    lax.fori_loop(0, tm, issue, 0, unroll=8)
    for s in range(2):
        pltpu.make_async_copy(ys_ref.at[pl.ds(0, tm * ROW_SLABS), :], ybuf_ref.at[s], sem).wait()
    info = info_ref[...]
    lane = lax.broadcasted_iota(jnp.int32, info.shape, 1)
    w1 = jnp.sum(jnp.where(lane == 4, info, 0.0), axis=1, keepdims=True)
    w2 = jnp.sum(jnp.where(lane == 5, info, 0.0), axis=1, keepdims=True)
    y = w1 * _from_token_major(ybuf_ref.at[0]) + w2 * _from_token_major(ybuf_ref.at[1])
    o_ref[...] = _ln(ALPHA * x_ref[...] + y, g_ref[...], b_ref[...])


def _combine_ln(x2, info, pos, ys, gamma, beta):
    n, d = x2.shape
    tm = ROW_TILE
    return pl.pallas_call(
        _combine_ln_kernel,
        grid=(n // tm,),
        in_specs=[pl.BlockSpec((None, 1, 2 * tm), lambda i: (i, 0, 0), memory_space=pltpu.SMEM),
                  pl.BlockSpec((tm, d), lambda i: (i, 0)),
                  pl.BlockSpec((tm, LANES), lambda i: (i, 0)),
                  pl.BlockSpec(memory_space=pl.ANY),
                  pl.BlockSpec((1, d), lambda i: (0, 0)),
                  pl.BlockSpec((1, d), lambda i: (0, 0))],
        out_specs=pl.BlockSpec((tm, d), lambda i: (i, 0)),
        out_shape=jax.ShapeDtypeStruct((n, d), F32),
        scratch_shapes=[pltpu.VMEM((2, tm * ROW_SLABS, LANES), F32), pltpu.SemaphoreType.DMA],
        compiler_params=_params(("arbitrary",)),
        name="moe_combine_ln",
    )(pos, x2, info, ys, gamma, beta)


def _moe_ln(x2, w_router3, layer, wg, wu, wd, gamma, beta):
    n = x2.shape[0]
    n_tiles = 2 * n // MOE_ROW_TILE + N_EXPERTS
    info, counts = _router(x2, w_router3)
    pos, owner, n_used = _route_plan(info, counts, n_tiles)
    xs = _dispatch(x2, pos, n_tiles * MOE_ROW_TILE)
    ys = _moe_ffn(xs, owner, n_used, layer, wg, wu, wd)
    return _combine_ln(x2, info, pos, ys, gamma, beta)


def _pad_cols(w, width):
    return jnp.pad(w, ((0, 0), (0, width - w.shape[1])))


def _even_weights(w_in, b_forget):
    wq, wk, wv, wf, wu = jnp.split(
        w_in, [FOX_WIDTH, 2 * FOX_WIDTH, 3 * FOX_WIDTH, 3 * FOX_WIDTH + FOX_HEADS], axis=1)
    w_qk = jnp.concatenate([wq, wk], axis=1).astype(BF16)
    w_vt = wv.T.astype(BF16)
    w_f3 = _pad_cols(jnp.concatenate([wf, wf, wf], axis=1), LANES).astype(BF16)
    b_f3 = _pad_cols(jnp.concatenate([b_forget, b_forget, b_forget])[None, :], LANES)
    return w_qk, w_vt, w_f3, b_f3, wu.astype(BF16)


def _odd_weights(w_in):
    wq, wk, wv = jnp.split(w_in, 3, axis=1)
    return jnp.concatenate([wq, wk], axis=1).astype(BF16), wv.T.astype(BF16)


def _router_weights(w_router):
    return jnp.concatenate([_pad_cols(p.astype(F32), LANES).astype(BF16)
                            for p in _split3(w_router)], axis=1)


def kernel(x, ln1_g, ln1_b, ln2_g, ln2_b, ev_w_in, ev_b_forget, ev_pool_w, ev_pool_scale, ev_w_out,
           ev_ffn_gate, ev_ffn_up, ev_ffn_down, od_w_in, od_w_out, od_router, od_exp_gate, od_exp_up,
           od_exp_down):
    bsz, seq, d = x.shape
    n = bsz * seq
    x2 = x.reshape(n, d)
    for layer in range(DEPTH):
        li = layer // 2
        g1, b1 = ln1_g[layer][None, :], ln1_b[layer][None, :]
        g2, b2 = ln2_g[layer][None, :], ln2_b[layer][None, :]
        if layer % 2 == 0:
            w_qk, w_vt, w_f3, b_f3, w_u = _even_weights(ev_w_in[li], ev_b_forget[li])
            qk, vt, z, u = _proj(x2, seq, [(w_qk, "qk", BF16), (w_vt, "nt", BF16),
                                           (w_f3, "nn", F32), (w_u, "nn", F32)])
            cum3 = _fox_cum(z.reshape(bsz, seq, LANES), b_f3)
            att = _fox_attn(qk.reshape(bsz, seq, -1), cum3, vt)
            pooled = _pool(u.reshape(bsz, seq, POOL_WIDTH), ev_pool_w[li].astype(BF16),
                           ev_pool_scale[li][None, :])
            w_out = ev_w_out[li].astype(BF16)
            x2 = _outproj_ln([att.reshape(n, FOX_WIDTH), pooled.reshape(n, POOL_WIDTH)],
                             [w_out[:FOX_WIDTH], w_out[FOX_WIDTH:]], x2, g1, b1)
            x2 = _ffn_ln(x2, ev_ffn_gate[li].astype(BF16), ev_ffn_up[li].astype(BF16),
                         ev_ffn_down[li].astype(BF16), g2, b2)
        else:
            w_qk, w_vt = _odd_weights(od_w_in[li])
            qk, vt = _proj(x2, seq, [(w_qk, "qk", BF16), (w_vt, "nt", BF16)])
            att = _moba_attn(qk.reshape(bsz, seq, -1), vt)
            x2 = _outproj_ln([att.reshape(n, MOBA_WIDTH)], [od_w_out[li].astype(BF16)], x2, g1, b1)
            x2 = _moe_ln(x2, _router_weights(od_router[li]), li, od_exp_gate, od_exp_up, od_exp_down,
                         g2, b2)
    return x2.reshape(bsz, seq, d)
```

```python
import functools

import jax
import jax.numpy as jnp
from jax import lax
from jax.experimental import pallas as pl
from jax.experimental.pallas import tpu as pltpu

F32 = jnp.float32
BF16 = jnp.bfloat16

D_MODEL = 1024
DEPTH = 4
HEAD_DIM = 64
FOX_HEADS = 8
FOX_WIDTH = FOX_HEADS * HEAD_DIM
POOL_WINDOWS = (2, 4, 8, 16)
POOL_WIDTH = 512
POOL_GROUP_DIM = 128
MOBA_HEADS = 16
MOBA_WIDTH = MOBA_HEADS * HEAD_DIM
MOBA_BLOCK = 256
MOBA_TOPK = 3
D_FF = 3584
N_EXPERTS = 8
LN_EPS = 1e-5
ALPHA = (2 * DEPTH) ** 0.25
LOG2E = 1.4426950408889634
Q_LOG2_SCALE = HEAD_DIM ** -0.5 * LOG2E

LANES = 128
ROW_SLABS = D_MODEL // LANES
ATT_TILE = 512
ROW_TILE = 512
FFN_ROW_TILE = 1024
FFN_COL_TILE = 512
MOE_ROW_TILE = 1024
CUM_BLOCK = 256
NEG_BIG = -1e30
VMEM_LIMIT = 48 * 1024 * 1024

_NT = (((1,), (1,)), ((), ()))


def _params(sem):
    return pltpu.CompilerParams(dimension_semantics=sem, vmem_limit_bytes=VMEM_LIMIT)


def _split3(x):
    hi = x.astype(BF16)
    r = x - hi.astype(F32)
    mid = r.astype(BF16)
    lo = (r - mid.astype(F32)).astype(BF16)
    return hi, mid, lo


def _ln(y, g, b):
    mu = jnp.mean(y, axis=-1, keepdims=True)
    yc = y - mu
    var = jnp.mean(yc * yc, axis=-1, keepdims=True)
    return yc * lax.rsqrt(var + LN_EPS) * g + b


def _proj_kernel(*refs, kinds):
    x_ref = refs[0]
    w_refs = refs[1:1 + len(kinds)]
    o_refs = refs[1 + len(kinds):]
    xb = x_ref[...].astype(BF16)
    for kind, w_ref, o_ref in zip(kinds, w_refs, o_refs):
        if kind == "nt":
            m = w_ref.shape[0]
            for c0 in range(0, m, 512):
                c1 = min(c0 + 512, m)
                o_ref[0, 0, c0:c1, :] = lax.dot_general(
                    w_ref[c0:c1, :], xb, _NT, preferred_element_type=F32).astype(o_ref.dtype)
        else:
            m = w_ref.shape[1]
            for c0 in range(0, m, 512):
                c1 = min(c0 + 512, m)
                y = jnp.dot(xb, w_ref[:, c0:c1], preferred_element_type=F32)
                if kind == "qk" and c0 < m // 2:
                    y = y * Q_LOG2_SCALE
                o_ref[:, c0:c1] = y.astype(o_ref.dtype)


def _proj(x2, seq, specs):
    n, d = x2.shape
    tm = ROW_TILE
    tps = seq // tm
    kinds = tuple(k for _, k, _ in specs)
    in_specs = [pl.BlockSpec((tm, d), lambda i: (i, 0))]
    out_specs, out_shapes = [], []
    for w, kind, dt in specs:
        in_specs.append(pl.BlockSpec(w.shape, lambda i: (0, 0)))
        if kind == "nt":
            m = w.shape[0]
            out_shapes.append(jax.ShapeDtypeStruct((n // seq, tps, m, tm), dt))
            out_specs.append(pl.BlockSpec((1, 1, m, tm), lambda i: (i // tps, i % tps, 0, 0)))
        else:
            m = w.shape[1]
            out_shapes.append(jax.ShapeDtypeStruct((n, m), dt))
            out_specs.append(pl.BlockSpec((tm, m), lambda i: (i, 0)))
    return pl.pallas_call(
        functools.partial(_proj_kernel, kinds=kinds),
        grid=(n // tm,),
        in_specs=in_specs,
        out_specs=out_specs,
        out_shape=out_shapes,
        compiler_params=_params(("arbitrary",)),
        name="proj",
    )(x2, *[w for w, _, _ in specs])


def _fox_cum_kernel(z_ref, b_ref, o_ref):
    seq = z_ref.shape[1]
    blk = CUM_BLOCK
    row = lax.broadcasted_iota(jnp.int32, (blk, blk), 0)
    col = lax.broadcasted_iota(jnp.int32, (blk, blk), 1)
    tri = (col <= row).astype(BF16)
    lane = lax.broadcasted_iota(jnp.int32, (blk, LANES), 1)
    carry = jnp.zeros((1, LANES), F32)
    for t in range(seq // blk):
        logf = jax.nn.log_sigmoid(z_ref[0, t * blk:(t + 1) * blk, :] + b_ref[...])
        hi, mid, lo = _split3(logf)
        c = (jnp.dot(tri, hi, preferred_element_type=F32)
             + jnp.dot(tri, mid, preferred_element_type=F32)
             + jnp.dot(tri, lo, preferred_element_type=F32)) + carry
        carry = c[blk - 1:blk, :]
        chi, cmid, clo = _split3(c * LOG2E)
        o_ref[0, t * blk:(t + 1) * blk, :] = jnp.where(
            lane < FOX_HEADS, chi, jnp.where(lane < 2 * FOX_HEADS, cmid, clo))


def _fox_cum(z3, bias3):
    b, s, _ = z3.shape
    return pl.pallas_call(
        _fox_cum_kernel,
        grid=(b,),
        in_specs=[pl.BlockSpec((1, s, LANES), lambda i: (i, 0, 0)),
                  pl.BlockSpec((1, LANES), lambda i: (0, 0))],
        out_specs=pl.BlockSpec((1, s, LANES), lambda i: (i, 0, 0)),
        out_shape=jax.ShapeDtypeStruct((b, s, LANES), BF16),
        compiler_params=_params(("arbitrary",)),
        name="fox_cum",
    )(z3, bias3)


SUM_ROWS = 16
PIPE_UNROLL = 14


def _softmax_step(s, vt1, m, acc):
    m_new = jnp.maximum(m, jnp.max(s, axis=0, keepdims=True))
    p = jnp.exp2(s - m_new).astype(BF16)
    acc_new = jnp.exp2(m - m_new) * acc + jnp.dot(vt1, p, preferred_element_type=F32)
    return m_new, acc_new


def _attn_scratch(s, t):
    nt = s // t
    return [pltpu.VMEM((2, nt, t, LANES), BF16), pltpu.VMEM((2, nt, t, LANES), BF16),
            pltpu.VMEM((2, 2, t, t), F32), pltpu.VMEM((nt, 2, 1, t), F32),
            pltpu.VMEM((nt, 2, HEAD_DIM + SUM_ROWS, t), F32)]


def _visit_tables(ntile):
    pairs = [(q, q - d) for d in range(1, ntile) for q in range(d, ntile)]
    if not pairs:
        pairs = [(0, 0)]
    return (jnp.asarray([q for q, _ in pairs], jnp.int32), jnp.asarray([k for _, k in pairs], jnp.int32))


def _attn_sweep(vq_ref, vk_ref, vt_ref, o_ref, kaug_ref, qaug_ref, s_ref, m_ref, acc_ref):
    ntile, tq = qaug_ref.shape[1], qaug_ref.shape[2]
    nvis = ntile * (ntile - 1) // 2
    causal = (lax.broadcasted_iota(jnp.int32, (tq, tq), 0)
              <= lax.broadcasted_iota(jnp.int32, (tq, tq), 1))
    ones = jnp.ones((SUM_ROWS, tq), BF16)

    def produce(buf, qi, ki, diagonal=False):
        for hh in range(2):
            s = lax.dot_general(kaug_ref[hh, ki], qaug_ref[hh, qi], _NT, preferred_element_type=F32)
            s_ref[buf, hh] = jnp.where(causal, s, -jnp.inf) if diagonal else s

    def consume(buf, qi, ki):
        for hh in range(2):
            vt1 = jnp.concatenate([vt_ref[0, ki, HEAD_DIM * hh:HEAD_DIM * (hh + 1), :], ones], axis=0)
            m_ref[qi, hh], acc_ref[qi, hh] = _softmax_step(
                s_ref[buf, hh], vt1, m_ref[qi, hh], acc_ref[qi, hh])

    def pipeline(count, visit, diagonal):
        last = count - 1
        produce(0, *visit(0), diagonal)

        def body(a, c):
            for r in range(PIPE_UNROLL):
                v = PIPE_UNROLL * a + r
                produce((r + 1) % 2, *visit(jnp.minimum(v + 1, last)), diagonal)
                consume(r % 2, *visit(v))
            return c

        lax.fori_loop(0, count // PIPE_UNROLL, body, 0)
        for v in range(count - count % PIPE_UNROLL, count):
            if v < last:
                produce((v + 1) % 2, *visit(v + 1), diagonal)
            consume(v % 2, *visit(v))

    m_ref[...] = jnp.full(m_ref.shape, -jnp.inf, F32)
    acc_ref[...] = jnp.zeros(acc_ref.shape, F32)
    pipeline(ntile, lambda v: (v, v), True)
    if nvis:
        pipeline(nvis, lambda v: (vq_ref[v], vk_ref[v]), False)
    for qi in range(ntile):
        o = jnp.concatenate([acc_ref[qi, hh, 0:HEAD_DIM] / acc_ref[qi, hh, HEAD_DIM:HEAD_DIM + 1]
                             for hh in range(2)], axis=0)
        o_ref[0, qi * tq:(qi + 1) * tq, :] = o.T.astype(o_ref.dtype)


def _attn_call(kernel_fn, name, b, s, npair, width, operands, in_specs, extra_scratch=()):
    t = ATT_TILE
    vq, vk = _visit_tables(s // t)
    smem = pl.BlockSpec(memory_space=pltpu.SMEM)
    return pl.pallas_call(
        kernel_fn,
        grid=(b, npair),
        in_specs=[smem, smem] + in_specs,
        out_specs=pl.BlockSpec((1, s, LANES), lambda b_, j: (b_, 0, j)),
        out_shape=jax.ShapeDtypeStruct((b, s, width), BF16),
        scratch_shapes=list(extra_scratch) + _attn_scratch(s, t),
        compiler_params=_params(("arbitrary", "arbitrary")),
        name=name,
    )(vq, vk, *operands)


def _fox_attn_kernel(vq_ref, vk_ref, q_ref, k_ref, c_ref, vt_ref, o_ref,
                     kaug_ref, qaug_ref, s_ref, m_ref, acc_ref):
    ntile, tq = qaug_ref.shape[1], qaug_ref.shape[2]
    j = pl.program_id(1)
    lane = lax.broadcasted_iota(jnp.int32, (1, LANES), 1)
    prow = lax.broadcasted_iota(jnp.int32, (LANES, LANES), 0)
    pcol = lax.broadcasted_iota(jnp.int32, (LANES, LANES), 1)

    def place(h, first_lane, sign):
        hit = (((prow == h) & (pcol == first_lane))
               | ((prow == h + FOX_HEADS) & (pcol == first_lane + 1))
               | ((prow == h + 2 * FOX_HEADS) & (pcol == first_lane + 2)))
        return jnp.where(hit, sign, 0.0).astype(BF16)

    def augment(x, c3, h, hh, gate_off, sign, ones_off):
        base = HEAD_DIM * (1 - hh)
        head_lanes = (lane >= HEAD_DIM * hh) & (lane < HEAD_DIM * (hh + 1))
        ones = (lane >= base + ones_off) & (lane < base + ones_off + 3)
        aug = jnp.dot(c3, place(h, base + gate_off, sign), preferred_element_type=F32)
        return jnp.where(head_lanes, x, jnp.where(ones, 1.0, aug).astype(BF16))

    for hh in range(2):
        for t in range(ntile):
            rows = slice(t * tq, (t + 1) * tq)
            kaug_ref[hh, t] = augment(k_ref[0, rows, :], c_ref[0, rows, :], 2 * j + hh, hh, 0, -1.0, 3)
            qaug_ref[hh, t] = augment(q_ref[0, rows, :], c_ref[0, rows, :], 2 * j + hh, hh, 3, 1.0, 0)

    _attn_sweep(vq_ref, vk_ref, vt_ref, o_ref, kaug_ref, qaug_ref, s_ref, m_ref, acc_ref)


def _fox_attn(qk3, cum3, vt4):
    b, s, _ = qk3.shape
    t = ATT_TILE
    npair = FOX_HEADS // 2
    in_specs = [pl.BlockSpec((1, s, LANES), lambda b_, j: (b_, 0, j)),
                pl.BlockSpec((1, s, LANES), lambda b_, j: (b_, 0, npair + j)),
                pl.BlockSpec((1, s, LANES), lambda b_, j: (b_, 0, 0)),
                pl.BlockSpec((1, s // t, LANES, t), lambda b_, j: (b_, 0, j, 0))]
    return _attn_call(_fox_attn_kernel, "fox_attn", b, s, npair, FOX_WIDTH,
                      (qk3, qk3, cum3, vt4), in_specs)


def _moba_attn_kernel(vq_ref, vk_ref, q_ref, k_ref, vt_ref, o_ref,
                      km_ref, kaug_ref, qaug_ref, s_ref, m_ref, acc_ref):
    ntile, tq = qaug_ref.shape[1], qaug_ref.shape[2]
    seq = k_ref.shape[1]
    nb = seq // MOBA_BLOCK
    nbp = km_ref.shape[1] // 3
    bpt = tq // MOBA_BLOCK
    blk_shift = MOBA_BLOCK.bit_length() - 1
    assert nbp <= HEAD_DIM
    lane = lax.broadcasted_iota(jnp.int32, (1, LANES), 1)

    rows = [jnp.sum(k_ref[0, n * MOBA_BLOCK:(n + 1) * MOBA_BLOCK, :].astype(F32),
                    axis=0, keepdims=True) * (1.0 / MOBA_BLOCK) for n in range(nb)]
    if nbp > nb:
        rows.append(jnp.zeros((nbp - nb, LANES), F32))
    kmean = jnp.concatenate(rows, axis=0)
    key_blk = lax.shift_right_logical(lax.broadcasted_iota(jnp.int32, (tq, LANES), 0), blk_shift)
    lane2 = lax.broadcasted_iota(jnp.int32, (tq, LANES), 1)
    nidx = lax.broadcasted_iota(jnp.int32, (nbp, tq), 0)
    qcol = lax.broadcasted_iota(jnp.int32, (nbp, tq), 1)
    for hh in range(2):
        head_lanes = (lane >= HEAD_DIM * hh) & (lane < HEAD_DIM * (hh + 1))
        base = HEAD_DIM * (1 - hh)
        km_ref[hh] = jnp.concatenate(_split3(jnp.where(head_lanes, kmean, 0.0)), axis=0)
        for t in range(ntile):
            onehot = jnp.where(lane2 - base == key_blk + t * bpt, 1.0, 0.0)
            kaug_ref[hh, t] = jnp.where(head_lanes, k_ref[0, t * tq:(t + 1) * tq, :],
                                        onehot.astype(BF16))
        for t in range(ntile):
            q = q_ref[0, t * tq:(t + 1) * tq, :]
            qblk = lax.shift_right_logical(t * tq + qcol, blk_shift)
            past = nidx < qblk
            qh = jnp.where(head_lanes, q, jnp.zeros_like(q))
            g3 = lax.dot_general(km_ref[hh], qh, _NT, preferred_element_type=F32)
            gate = g3[0:nbp] + g3[nbp:2 * nbp] + g3[2 * nbp:3 * nbp]
            g = jnp.where(past, gate, -jnp.inf)
            sel = nidx == qblk
            for _ in range(MOBA_TOPK):
                first = jnp.min(jnp.where(g == jnp.max(g, axis=0, keepdims=True), nidx, nbp),
                                axis=0, keepdims=True)
                pick = nidx == first
                sel = sel | (pick & past)
                g = jnp.where(pick, -jnp.inf, g)
            parts = [jnp.zeros((base, tq), F32), jnp.where(sel, 0.0, NEG_BIG),
                     jnp.zeros((LANES - base - nbp, tq), F32)]
            bias = jnp.concatenate([r for r in parts if r.shape[0]], axis=0)
            qaug_ref[hh, t] = jnp.where(head_lanes, q, bias.T.astype(BF16))

    _attn_sweep(vq_ref, vk_ref, vt_ref, o_ref, kaug_ref, qaug_ref, s_ref, m_ref, acc_ref)


def _moba_attn(qk3, vt4):
    b, s, _ = qk3.shape
    t = ATT_TILE
    npair = MOBA_HEADS // 2
    nbp = -(-(s // MOBA_BLOCK) // 16) * 16
    in_specs = [pl.BlockSpec((1, s, LANES), lambda b_, j: (b_, 0, j)),
                pl.BlockSpec((1, s, LANES), lambda b_, j: (b_, 0, npair + j)),
                pl.BlockSpec((1, s // t, LANES, t), lambda b_, j: (b_, 0, j, 0))]
    return _attn_call(_moba_attn_kernel, "moba_attn", b, s, npair, MOBA_WIDTH,
                      (qk3, qk3, vt4), in_specs,
                      extra_scratch=[pltpu.VMEM((2, 3 * nbp, LANES), BF16)])


def _pool_kernel(u_ref, w_ref, sc_ref, o_ref):
    g = pl.program_id(1)
    u = u_ref[0]
    t = lax.broadcasted_iota(jnp.int32, u.shape, 0)
    wsum = u
    acc = jnp.zeros_like(u)
    for r, w in enumerate(POOL_WINDOWS):
        d = w // 2
        wsum = wsum + jnp.where(t >= d, pltpu.roll(wsum, d, 0), 0.0)
        cnt = jnp.minimum(t + 1, w).astype(F32)
        acc = jnp.where(g == r, wsum / cnt, acc)
    mixed = (acc - u).astype(BF16)
    y = jnp.dot(mixed, w_ref[0], preferred_element_type=F32) * sc_ref[...]
    o_ref[0] = y.astype(o_ref.dtype)


def _pool(u3, pool_w, pool_scale):
    b, s, _ = u3.shape
    ng = len(POOL_WINDOWS)
    return pl.pallas_call(
        _pool_kernel,
        grid=(b, ng),
        in_specs=[pl.BlockSpec((1, s, LANES), lambda b_, g: (b_, 0, g)),
                  pl.BlockSpec((1, POOL_GROUP_DIM, POOL_GROUP_DIM), lambda b_, g: (g, 0, 0)),
                  pl.BlockSpec((1, LANES), lambda b_, g: (0, g))],
        out_specs=pl.BlockSpec((1, s, LANES), lambda b_, g: (b_, 0, g)),
        out_shape=jax.ShapeDtypeStruct((b, s, POOL_WIDTH), BF16),
        compiler_params=_params(("arbitrary", "arbitrary")),
        name="pool",
    )(u3, pool_w, pool_scale)


def _outproj_ln_kernel(*refs, n_in):
    a_refs = refs[:n_in]
    w_refs = refs[n_in:2 * n_in]
    x_ref, g_ref, b_ref, o_ref = refs[2 * n_in:]
    mix = jnp.dot(a_refs[0][...], w_refs[0][...], preferred_element_type=F32)
    for a_ref, w_ref in zip(a_refs[1:], w_refs[1:]):
        mix = mix + jnp.dot(a_ref[...], w_ref[...], preferred_element_type=F32)
    o_ref[...] = _ln(ALPHA * x_ref[...] + mix, g_ref[...], b_ref[...])


def _outproj_ln(acts, weights, x2, gamma, beta):
    n, d = x2.shape
    tm = ROW_TILE
    in_specs = [pl.BlockSpec((tm, a.shape[1]), lambda i: (i, 0)) for a in acts]
    in_specs += [pl.BlockSpec(w.shape, lambda i: (0, 0)) for w in weights]
    in_specs += [pl.BlockSpec((tm, d), lambda i: (i, 0)),
                 pl.BlockSpec((1, d), lambda i: (0, 0)),
                 pl.BlockSpec((1, d), lambda i: (0, 0))]
    return pl.pallas_call(
        functools.partial(_outproj_ln_kernel, n_in=len(acts)),
        grid=(n // tm,),
        in_specs=in_specs,
        out_specs=pl.BlockSpec((tm, d), lambda i: (i, 0)),
        out_shape=jax.ShapeDtypeStruct((n, d), F32),
        compiler_params=_params(("arbitrary",)),
        name="outproj_ln",
    )(*acts, *weights, x2, gamma, beta)


def _swiglu_chunk(xb, wg_ref, wu_ref, wd_ref):
    hid = (jax.nn.silu(jnp.dot(xb, wg_ref[...].astype(BF16), preferred_element_type=F32))
           * jnp.dot(xb, wu_ref[...].astype(BF16), preferred_element_type=F32))
    return jnp.dot(hid.astype(BF16), wd_ref[...].astype(BF16), preferred_element_type=F32)


def _ffn_kernel(x_ref, wg_ref, wu_ref, wd_ref, g_ref, b_ref, o_ref, acc_ref, xb_ref):
    f = pl.program_id(1)

    @pl.when(f == 0)
    def _():
        acc_ref[...] = jnp.zeros_like(acc_ref)
        xb_ref[...] = x_ref[...].astype(BF16)

    acc_ref[...] += _swiglu_chunk(xb_ref[...], wg_ref, wu_ref, wd_ref)

    @pl.when(f == pl.num_programs(1) - 1)
    def _():
        o_ref[...] = _ln(ALPHA * x_ref[...] + acc_ref[...], g_ref[...], b_ref[...])


def _ffn_ln(x2, wg, wu, wd, gamma, beta):
    n, d = x2.shape
    tm, tf = FFN_ROW_TILE, FFN_COL_TILE
    ff = wg.shape[1]
    return pl.pallas_call(
        _ffn_kernel,
        grid=(n // tm, ff // tf),
        in_specs=[pl.BlockSpec((tm, d), lambda i, f: (i, 0)),
                  pl.BlockSpec((d, tf), lambda i, f: (0, f)),
                  pl.BlockSpec((d, tf), lambda i, f: (0, f)),
                  pl.BlockSpec((tf, d), lambda i, f: (f, 0)),
                  pl.BlockSpec((1, d), lambda i, f: (0, 0)),
                  pl.BlockSpec((1, d), lambda i, f: (0, 0))],
        out_specs=pl.BlockSpec((tm, d), lambda i, f: (i, 0)),
        out_shape=jax.ShapeDtypeStruct((n, d), F32),
        scratch_shapes=[pltpu.VMEM((tm, d), F32), pltpu.VMEM((tm, d), BF16)],
        compiler_params=_params(("arbitrary", "arbitrary")),
        name="ffn_ln",
    )(x2, wg, wu, wd, gamma, beta)


def _router_kernel(x_ref, w_ref, info_ref, cnt_ref, seen_ref):
    @pl.when(pl.program_id(0) == 0)
    def _():
        seen_ref[...] = jnp.zeros_like(seen_ref)

    x = x_ref[...]
    tm = x.shape[0]
    xh = x.astype(BF16)
    xm = (x - xh.astype(F32)).astype(BF16)
    l3 = (jnp.dot(xh, w_ref[...], preferred_element_type=F32)
          + jnp.dot(xm, w_ref[...], preferred_element_type=F32))
    logits = l3[:, 0:LANES] + l3[:, LANES:2 * LANES] + l3[:, 2 * LANES:3 * LANES]
    lane = lax.broadcasted_iota(jnp.int32, logits.shape, 1)
    g = jnp.where(lane < N_EXPERTS, logits, -jnp.inf)
    v1 = jnp.max(g, axis=1, keepdims=True)
    i1 = jnp.min(jnp.where(g == v1, lane, LANES), axis=1, keepdims=True)
    p1 = lane == i1
    g = jnp.where(p1, -jnp.inf, g)
    v2 = jnp.max(g, axis=1, keepdims=True)
    i2 = jnp.min(jnp.where(g == v2, lane, LANES), axis=1, keepdims=True)
    p2 = lane == i2
    e2 = jnp.exp(v2 - v1)
    den = 1.0 + e2
    member = jnp.where(p1 | p2, 1.0, 0.0)
    earlier = (lax.broadcasted_iota(jnp.int32, (tm, tm), 1)
               < lax.broadcasted_iota(jnp.int32, (tm, tm), 0)).astype(BF16)
    before = jnp.dot(earlier, member.astype(BF16), preferred_element_type=F32) + seen_ref[...]
    r1 = jnp.sum(jnp.where(p1, before, 0.0), axis=1, keepdims=True)
    r2 = jnp.sum(jnp.where(p2, before, 0.0), axis=1, keepdims=True)
    seen_ref[...] += jnp.sum(member, axis=0, keepdims=True)
    cnt_ref[...] = seen_ref[...]
    cols = (i1.astype(F32), i2.astype(F32), r1, r2, 1.0 / den, e2 / den)
    info = jnp.zeros_like(logits)
    for c, v in enumerate(cols):
        info = jnp.where(lane == c, v, info)
    info_ref[...] = info


def _router(x2, w3):
    n, d = x2.shape
    tm = ROW_TILE
    return pl.pallas_call(
        _router_kernel,
        grid=(n // tm,),
        in_specs=[pl.BlockSpec((tm, d), lambda i: (i, 0)),
                  pl.BlockSpec(w3.shape, lambda i: (0, 0))],
        out_specs=[pl.BlockSpec((tm, LANES), lambda i: (i, 0)),
                   pl.BlockSpec((1, LANES), lambda i: (0, 0))],
        out_shape=[jax.ShapeDtypeStruct((n, LANES), F32),
                   jax.ShapeDtypeStruct((1, LANES), F32)],
        scratch_shapes=[pltpu.VMEM((1, LANES), F32)],
        compiler_params=_params(("arbitrary",)),
        name="router",
    )(x2, w3)


def _route_plan(info, counts, n_tiles):
    tmf = MOE_ROW_TILE
    cnt = counts[0, :N_EXPERTS].astype(jnp.int32)
    padded = (cnt + tmf - 1) // tmf * tmf
    ends = jnp.cumsum(padded)
    offs = ends - padded
    eid = info[:, 0:2].astype(jnp.int32)
    rank = info[:, 2:4].astype(jnp.int32)
    pos = rank + jnp.sum(jnp.where(eid[..., None] == jnp.arange(N_EXPERTS), offs, 0), axis=-1)
    n_used = ends[-1] // tmf
    tile = jnp.arange(n_tiles)
    owner = jnp.minimum(jnp.sum(tile[:, None] * tmf >= ends[None, :], axis=1), N_EXPERTS - 1)
    owner = jnp.where(tile < n_used, owner, owner[n_used - 1])
    return (pos.reshape(-1, 1, 2 * ROW_TILE), owner.astype(jnp.int32),
            n_used.reshape(1).astype(jnp.int32))


def _to_token_major(dst_ref, val):
    tm = val.shape[0]
    for s in range(ROW_SLABS):
        dst_ref[pl.ds(s, tm, stride=ROW_SLABS), :] = val[:, s * LANES:(s + 1) * LANES]


def _from_token_major(src_ref):
    tm = src_ref.shape[0] // ROW_SLABS
    return jnp.concatenate([src_ref[pl.ds(s, tm, stride=ROW_SLABS), :] for s in range(ROW_SLABS)],
                           axis=1)


def _token_rows(ref, t):
    return ref.at[pl.ds(pl.multiple_of(t * ROW_SLABS, ROW_SLABS), ROW_SLABS), :]


def _dispatch_kernel(pos_ref, x_ref, zero_ref, xs_ref, xt_ref, sem):
    del zero_ref
    tm = x_ref.shape[0]
    _to_token_major(xt_ref, x_ref[...])

    def row_copy(t, s):
        return pltpu.make_async_copy(_token_rows(xt_ref, t),
                                     _token_rows(xs_ref, pos_ref[0, 2 * t + s]), sem)

    def issue(t, c):
        row_copy(t, 0).start()
        row_copy(t, 1).start(priority=1)
        return c

    lax.fori_loop(0, tm, issue, 0, unroll=8)
    for _ in range(2):
        pltpu.make_async_copy(xt_ref, xs_ref.at[pl.ds(0, tm * ROW_SLABS), :], sem).wait()


def _dispatch(x2, pos, n_rows):
    n, d = x2.shape
    tm = ROW_TILE
    return pl.pallas_call(
        _dispatch_kernel,
        grid=(n // tm,),
        in_specs=[pl.BlockSpec((None, 1, 2 * tm), lambda i: (i, 0, 0), memory_space=pltpu.SMEM),
                  pl.BlockSpec((tm, d), lambda i: (i, 0)),
                  pl.BlockSpec(memory_space=pl.ANY)],
        out_specs=pl.BlockSpec(memory_space=pl.ANY),
        out_shape=jax.ShapeDtypeStruct((n_rows * ROW_SLABS, LANES), F32),
        input_output_aliases={2: 0},
        scratch_shapes=[pltpu.VMEM((tm * ROW_SLABS, LANES), F32), pltpu.SemaphoreType.DMA],
        compiler_params=_params(("arbitrary",)),
        name="moe_dispatch",
    )(pos, x2, jnp.zeros((n_rows * ROW_SLABS, LANES), F32))


def _moe_ffn_kernel(owner_ref, used_ref, xs_ref, wg_ref, wu_ref, wd_ref, ys_ref, acc_ref, xb_ref):
    del owner_ref
    i, f = pl.program_id(0), pl.program_id(1)
    used = i < used_ref[0]

    @pl.when(f == 0)
    def _():
        acc_ref[...] = jnp.zeros_like(acc_ref)
        xb_ref[...] = _from_token_major(xs_ref).astype(BF16)

    @pl.when(used)
    def _():
        acc_ref[...] += _swiglu_chunk(xb_ref[...], wg_ref, wu_ref, wd_ref)

    @pl.when(f == pl.num_programs(1) - 1)
    def _():
        _to_token_major(ys_ref, acc_ref[...])


def _moe_ffn(xs, owner, n_used, layer, wg, wu, wd):
    d = wg.shape[2]
    n_rows = xs.shape[0] // ROW_SLABS
    tm, tf = MOE_ROW_TILE, FFN_COL_TILE
    ff = wg.shape[3]
    nf = ff // tf
    col = lambda i, f, used: jnp.where(i < used[0], f, nf - 1)
    return pl.pallas_call(
        _moe_ffn_kernel,
        grid_spec=pltpu.PrefetchScalarGridSpec(
            num_scalar_prefetch=2,
            grid=(n_rows // tm, nf),
            in_specs=[
                pl.BlockSpec((tm * ROW_SLABS, LANES),
                             lambda i, f, own, used: (jnp.minimum(i, used[0] - 1), 0)),
                pl.BlockSpec((None, None, d, tf),
                             lambda i, f, own, used: (layer, own[i], 0, col(i, f, used))),
                pl.BlockSpec((None, None, d, tf),
                             lambda i, f, own, used: (layer, own[i], 0, col(i, f, used))),
                pl.BlockSpec((None, None, tf, d),
                             lambda i, f, own, used: (layer, own[i], col(i, f, used), 0)),
            ],
            out_specs=pl.BlockSpec((tm * ROW_SLABS, LANES), lambda i, f, own, used: (i, 0)),
            scratch_shapes=[pltpu.VMEM((tm, d), F32), pltpu.VMEM((tm, d), BF16)],
        ),
        out_shape=jax.ShapeDtypeStruct((n_rows * ROW_SLABS, LANES), F32),
        compiler_params=_params(("arbitrary", "arbitrary")),
        name="moe_ffn",
    )(owner, n_used, xs, wg, wu, wd)


def _combine_ln_kernel(pos_ref, pos_next_ref, x_ref, info_ref, ys_ref, g_ref, b_ref, o_ref,
                       ybuf_ref, sems):
    tm = x_ref.shape[0]
    i = pl.program_id(0)
    slot = lax.rem(i, 2)

    def gather(p_ref, sl):
        def row_copy(t, s):
            return pltpu.make_async_copy(_token_rows(ys_ref, p_ref[0, 2 * t + s]),
                                         _token_rows(ybuf_ref.at[sl, s], t), sems.at[sl])

        def issue(t, c):
            row_copy(t, 0).start()
            row_copy(t, 1).start(priority=1)
            return c

        lax.fori_loop(0, tm, issue, 0, unroll=8)

    @pl.when(i == 0)
    def _():
        gather(pos_ref, 0)

    @pl.when(i + 1 < pl.num_programs(0))
    def _():
        gather(pos_next_ref, 1 - slot)

    for s in range(2):
        pltpu.make_async_copy(ys_ref.at[pl.ds(0, tm * ROW_SLABS), :], ybuf_ref.at[slot, s],
                              sems.at[slot]).wait()
    info = info_ref[...]
    lane = lax.broadcasted_iota(jnp.int32, info.shape, 1)
    w1 = jnp.sum(jnp.where(lane == 4, info, 0.0), axis=1, keepdims=True)
    w2 = jnp.sum(jnp.where(lane == 5, info, 0.0), axis=1, keepdims=True)
    y = (w1 * _from_token_major(ybuf_ref.at[slot, 0]) + w2 * _from_token_major(ybuf_ref.at[slot, 1]))
    o_ref[...] = _ln(ALPHA * x_ref[...] + y, g_ref[...], b_ref[...])


def _combine_ln(x2, info, pos, ys, gamma, beta):
    n, d = x2.shape
    tm = ROW_TILE
    last = n // tm - 1
    return pl.pallas_call(
        _combine_ln_kernel,
        grid=(n // tm,),
        in_specs=[pl.BlockSpec((None, 1, 2 * tm), lambda i: (i, 0, 0), memory_space=pltpu.SMEM),
                  pl.BlockSpec((None, 1, 2 * tm), lambda i: (jnp.minimum(i + 1, last), 0, 0),
                               memory_space=pltpu.SMEM),
                  pl.BlockSpec((tm, d), lambda i: (i, 0)),
                  pl.BlockSpec((tm, LANES), lambda i: (i, 0)),
                  pl.BlockSpec(memory_space=pl.ANY),
                  pl.BlockSpec((1, d), lambda i: (0, 0)),
                  pl.BlockSpec((1, d), lambda i: (0, 0))],
        out_specs=pl.BlockSpec((tm, d), lambda i: (i, 0)),
        out_shape=jax.ShapeDtypeStruct((n, d), F32),
        scratch_shapes=[pltpu.VMEM((2, 2, tm * ROW_SLABS, LANES), F32),
                        pltpu.SemaphoreType.DMA((2,))],
        compiler_params=_params(("arbitrary",)),
        name="moe_combine_ln",
    )(pos, pos, x2, info, ys, gamma, beta)


def _moe_ln(x2, w_router3, layer, wg, wu, wd, gamma, beta):
    n = x2.shape[0]
    n_tiles = 2 * n // MOE_ROW_TILE + N_EXPERTS
    info, counts = _router(x2, w_router3)
    pos, owner, n_used = _route_plan(info, counts, n_tiles)
    xs = _dispatch(x2, pos, n_tiles * MOE_ROW_TILE)
    ys = _moe_ffn(xs, owner, n_used, layer, wg, wu, wd)
    return _combine_ln(x2, info, pos, ys, gamma, beta)


def _pad_cols(w, width):
    return jnp.pad(w, ((0, 0), (0, width - w.shape[1])))


def _even_weights(w_in, b_forget):
    wq, wk, wv, wf, wu = jnp.split(
        w_in, [FOX_WIDTH, 2 * FOX_WIDTH, 3 * FOX_WIDTH, 3 * FOX_WIDTH + FOX_HEADS], axis=1)
    w_qk = jnp.concatenate([wq, wk], axis=1).astype(BF16)
    w_vt = wv.T.astype(BF16)
    w_f3 = _pad_cols(jnp.concatenate([wf, wf, wf], axis=1), LANES).astype(BF16)
    b_f3 = _pad_cols(jnp.concatenate([b_forget, b_forget, b_forget])[None, :], LANES)
    return w_qk, w_vt, w_f3, b_f3, wu.astype(BF16)


def _odd_weights(w_in):
    wq, wk, wv = jnp.split(w_in, 3, axis=1)
    return jnp.concatenate([wq, wk], axis=1).astype(BF16), wv.T.astype(BF16)


def _router_weights(w_router):
    return jnp.concatenate([_pad_cols(p.astype(F32), LANES).astype(BF16)
                            for p in _split3(w_router)], axis=1)


def kernel(x, ln1_g, ln1_b, ln2_g, ln2_b, ev_w_in, ev_b_forget, ev_pool_w, ev_pool_scale, ev_w_out,
           ev_ffn_gate, ev_ffn_up, ev_ffn_down, od_w_in, od_w_out, od_router, od_exp_gate, od_exp_up,
           od_exp_down):
    bsz, seq, d = x.shape
    n = bsz * seq
    x2 = x.reshape(n, d)
    for layer in range(DEPTH):
        li = layer // 2
        g1, b1 = ln1_g[layer][None, :], ln1_b[layer][None, :]
        g2, b2 = ln2_g[layer][None, :], ln2_b[layer][None, :]
        if layer % 2 == 0:
            w_qk, w_vt, w_f3, b_f3, w_u = _even_weights(ev_w_in[li], ev_b_forget[li])
            qk, vt, z, u = _proj(x2, seq, [(w_qk, "qk", BF16), (w_vt, "nt", BF16),
                                           (w_f3, "nn", F32), (w_u, "nn", F32)])
            cum3 = _fox_cum(z.reshape(bsz, seq, LANES), b_f3)
            att = _fox_attn(qk.reshape(bsz, seq, -1), cum3, vt)
            pooled = _pool(u.reshape(bsz, seq, POOL_WIDTH), ev_pool_w[li].astype(BF16),
                           ev_pool_scale[li][None, :])
            w_out = ev_w_out[li].astype(BF16)
            x2 = _outproj_ln([att.reshape(n, FOX_WIDTH), pooled.reshape(n, POOL_WIDTH)],
                             [w_out[:FOX_WIDTH], w_out[FOX_WIDTH:]], x2, g1, b1)
            x2 = _ffn_ln(x2, ev_ffn_gate[li].astype(BF16), ev_ffn_up[li].astype(BF16),
                         ev_ffn_down[li].astype(BF16), g2, b2)
        else:
            w_qk, w_vt = _odd_weights(od_w_in[li])
            qk, vt = _proj(x2, seq, [(w_qk, "qk", BF16), (w_vt, "nt", BF16)])
            att = _moba_attn(qk.reshape(bsz, seq, -1), vt)
            x2 = _outproj_ln([att.reshape(n, MOBA_WIDTH)], [od_w_out[li].astype(BF16)], x2, g1, b1)
            x2 = _moe_ln(x2, _router_weights(od_router[li]), li, od_exp_gate, od_exp_up, od_exp_down,
                         g2, b2)
    return x2.reshape(bsz, seq, d)
```

```python
import functools

import jax
import jax.numpy as jnp
from jax import lax
from jax.experimental import pallas as pl
from jax.experimental.pallas import tpu as pltpu

F32 = jnp.float32
BF16 = jnp.bfloat16

D_MODEL = 1024
DEPTH = 4
HEAD_DIM = 64
FOX_HEADS = 8
FOX_WIDTH = FOX_HEADS * HEAD_DIM
POOL_WINDOWS = (2, 4, 8, 16)
POOL_WIDTH = 512
POOL_GROUP_DIM = 128
MOBA_HEADS = 16
MOBA_WIDTH = MOBA_HEADS * HEAD_DIM
MOBA_BLOCK = 256
MOBA_TOPK = 3
D_FF = 3584
N_EXPERTS = 8
LN_EPS = 1e-5
ALPHA = (2 * DEPTH) ** 0.25
LOG2E = 1.4426950408889634
Q_LOG2_SCALE = HEAD_DIM ** -0.5 * LOG2E

LANES = 128
ROW_SLABS = D_MODEL // LANES
ATT_TILE = 512
ROW_TILE = 512
FFN_ROW_TILE = 1024
FFN_COL_TILE = 512
MOE_ROW_TILE = 1024
CUM_BLOCK = 256
NEG_BIG = -1e30
VMEM_LIMIT = 48 * 1024 * 1024

_NT = (((1,), (1,)), ((), ()))


def _params(sem):
    return pltpu.CompilerParams(dimension_semantics=sem, vmem_limit_bytes=VMEM_LIMIT)


def _split3(x):
    hi = x.astype(BF16)
    r = x - hi.astype(F32)
    mid = r.astype(BF16)
    lo = (r - mid.astype(F32)).astype(BF16)
    return hi, mid, lo


def _ln(y, g, b):
    mu = jnp.mean(y, axis=-1, keepdims=True)
    yc = y - mu
    var = jnp.mean(yc * yc, axis=-1, keepdims=True)
    return yc * lax.rsqrt(var + LN_EPS) * g + b


def _proj_kernel(*refs, kinds):
    x_ref = refs[0]
    w_refs = refs[1:1 + len(kinds)]
    o_refs = refs[1 + len(kinds):]
    xb = x_ref[...].astype(BF16)
    for kind, w_ref, o_ref in zip(kinds, w_refs, o_refs):
        if kind == "nt":
            m = w_ref.shape[0]
            for c0 in range(0, m, 512):
                c1 = min(c0 + 512, m)
                o_ref[0, 0, c0:c1, :] = lax.dot_general(
                    w_ref[c0:c1, :], xb, _NT, preferred_element_type=F32).astype(o_ref.dtype)
        else:
            m = w_ref.shape[1]
            for c0 in range(0, m, 512):
                c1 = min(c0 + 512, m)
                y = jnp.dot(xb, w_ref[:, c0:c1], preferred_element_type=F32)
                if kind == "qk" and c0 < m // 2:
                    y = y * Q_LOG2_SCALE
                o_ref[:, c0:c1] = y.astype(o_ref.dtype)


def _proj(x2, seq, specs):
    n, d = x2.shape
    tm = ROW_TILE
    tps = seq // tm
    kinds = tuple(k for _, k, _ in specs)
    in_specs = [pl.BlockSpec((tm, d), lambda i: (i, 0))]
    out_specs, out_shapes = [], []
    for w, kind, dt in specs:
        in_specs.append(pl.BlockSpec(w.shape, lambda i: (0, 0)))
        if kind == "nt":
            m = w.shape[0]
            out_shapes.append(jax.ShapeDtypeStruct((n // seq, tps, m, tm), dt))
            out_specs.append(pl.BlockSpec((1, 1, m, tm), lambda i: (i // tps, i % tps, 0, 0)))
        else:
            m = w.shape[1]
            out_shapes.append(jax.ShapeDtypeStruct((n, m), dt))
            out_specs.append(pl.BlockSpec((tm, m), lambda i: (i, 0)))
    return pl.pallas_call(
        functools.partial(_proj_kernel, kinds=kinds),
        grid=(n // tm,),
        in_specs=in_specs,
        out_specs=out_specs,
        out_shape=out_shapes,
        compiler_params=_params(("arbitrary",)),
        name="proj",
    )(x2, *[w for w, _, _ in specs])


def _fox_cum_kernel(z_ref, b_ref, o_ref):
    seq = z_ref.shape[1]
    blk = CUM_BLOCK
    row = lax.broadcasted_iota(jnp.int32, (blk, blk), 0)
    col = lax.broadcasted_iota(jnp.int32, (blk, blk), 1)
    tri = (col <= row).astype(BF16)
    lane = lax.broadcasted_iota(jnp.int32, (blk, LANES), 1)
    carry = jnp.zeros((1, LANES), F32)
    for t in range(seq // blk):
        logf = jax.nn.log_sigmoid(z_ref[0, t * blk:(t + 1) * blk, :] + b_ref[...])
        hi, mid, lo = _split3(logf)
        c = (jnp.dot(tri, hi, preferred_element_type=F32)
             + jnp.dot(tri, mid, preferred_element_type=F32)
             + jnp.dot(tri, lo, preferred_element_type=F32)) + carry
        carry = c[blk - 1:blk, :]
        chi, cmid, clo = _split3(c * LOG2E)
        o_ref[0, t * blk:(t + 1) * blk, :] = jnp.where(
            lane < FOX_HEADS, chi, jnp.where(lane < 2 * FOX_HEADS, cmid, clo))


def _fox_cum(z3, bias3):
    b, s, _ = z3.shape
    return pl.pallas_call(
        _fox_cum_kernel,
        grid=(b,),
        in_specs=[pl.BlockSpec((1, s, LANES), lambda i: (i, 0, 0)),
                  pl.BlockSpec((1, LANES), lambda i: (0, 0))],
        out_specs=pl.BlockSpec((1, s, LANES), lambda i: (i, 0, 0)),
        out_shape=jax.ShapeDtypeStruct((b, s, LANES), BF16),
        compiler_params=_params(("arbitrary",)),
        name="fox_cum",
    )(z3, bias3)


SUM_ROWS = 16
PIPE_UNROLL = 14


def _softmax_step(s, vt1, m, acc):
    m_new = jnp.maximum(m, jnp.max(s, axis=0, keepdims=True))
    p = jnp.exp2(s - m_new).astype(BF16)
    acc_new = jnp.exp2(m - m_new) * acc + jnp.dot(vt1, p, preferred_element_type=F32)
    return m_new, acc_new


def _attn_scratch(s, t):
    nt = s // t
    return [pltpu.VMEM((2, nt, t, LANES), BF16), pltpu.VMEM((2, nt, t, LANES), BF16),
            pltpu.VMEM((2, 2, t, t), F32), pltpu.VMEM((nt, 2, 1, t), F32),
            pltpu.VMEM((nt, 2, HEAD_DIM + SUM_ROWS, t), F32)]


def _visit_tables(ntile):
    pairs = [(q, q - d) for d in range(1, ntile) for q in range(d, ntile)]
    if not pairs:
        pairs = [(0, 0)]
    return (jnp.asarray([q for q, _ in pairs], jnp.int32), jnp.asarray([k for _, k in pairs], jnp.int32))


def _attn_sweep(vq_ref, vk_ref, vt_ref, o_ref, kaug_ref, qaug_ref, s_ref, m_ref, acc_ref):
    ntile, tq = qaug_ref.shape[1], qaug_ref.shape[2]
    nvis = ntile * (ntile - 1) // 2
    causal = (lax.broadcasted_iota(jnp.int32, (tq, tq), 0)
              <= lax.broadcasted_iota(jnp.int32, (tq, tq), 1))
    ones = jnp.ones((SUM_ROWS, tq), BF16)

    def produce(buf, qi, ki, diagonal=False):
        for hh in range(2):
            s = lax.dot_general(kaug_ref[hh, ki], qaug_ref[hh, qi], _NT, preferred_element_type=F32)
            s_ref[buf, hh] = jnp.where(causal, s, -jnp.inf) if diagonal else s

    def consume(buf, qi, ki):
        for hh in range(2):
            vt1 = jnp.concatenate([vt_ref[0, ki, HEAD_DIM * hh:HEAD_DIM * (hh + 1), :], ones], axis=0)
            m_ref[qi, hh], acc_ref[qi, hh] = _softmax_step(
                s_ref[buf, hh], vt1, m_ref[qi, hh], acc_ref[qi, hh])

    def pipeline(count, visit, diagonal):
        last = count - 1
        produce(0, *visit(0), diagonal)

        def body(a, c):
            for r in range(PIPE_UNROLL):
                v = PIPE_UNROLL * a + r
                produce((r + 1) % 2, *visit(jnp.minimum(v + 1, last)), diagonal)
                consume(r % 2, *visit(v))
            return c

        lax.fori_loop(0, count // PIPE_UNROLL, body, 0)
        for v in range(count - count % PIPE_UNROLL, count):
            if v < last:
                produce((v + 1) % 2, *visit(v + 1), diagonal)
            consume(v % 2, *visit(v))

    m_ref[...] = jnp.full(m_ref.shape, -jnp.inf, F32)
    acc_ref[...] = jnp.zeros(acc_ref.shape, F32)
    pipeline(ntile, lambda v: (v, v), True)
    if nvis:
        pipeline(nvis, lambda v: (vq_ref[v], vk_ref[v]), False)
    for qi in range(ntile):
        o = jnp.concatenate([acc_ref[qi, hh, 0:HEAD_DIM] / acc_ref[qi, hh, HEAD_DIM:HEAD_DIM + 1]
                             for hh in range(2)], axis=0)
        o_ref[0, qi * tq:(qi + 1) * tq, :] = o.T.astype(o_ref.dtype)


def _attn_call(kernel_fn, name, b, s, npair, width, operands, in_specs, extra_scratch=()):
    t = ATT_TILE
    vq, vk = _visit_tables(s // t)
    smem = pl.BlockSpec(memory_space=pltpu.SMEM)
    return pl.pallas_call(
        kernel_fn,
        grid=(b, npair),
        in_specs=[smem, smem] + in_specs,
        out_specs=pl.BlockSpec((1, s, LANES), lambda b_, j: (b_, 0, j)),
        out_shape=jax.ShapeDtypeStruct((b, s, width), BF16),
        scratch_shapes=list(extra_scratch) + _attn_scratch(s, t),
        compiler_params=_params(("arbitrary", "arbitrary")),
        name=name,
    )(vq, vk, *operands)


def _fox_attn_kernel(vq_ref, vk_ref, q_ref, k_ref, c_ref, vt_ref, o_ref,
                     kaug_ref, qaug_ref, s_ref, m_ref, acc_ref):
    ntile, tq = qaug_ref.shape[1], qaug_ref.shape[2]
    j = pl.program_id(1)
    lane = lax.broadcasted_iota(jnp.int32, (1, LANES), 1)
    prow = lax.broadcasted_iota(jnp.int32, (LANES, LANES), 0)
    pcol = lax.broadcasted_iota(jnp.int32, (LANES, LANES), 1)

    def place(h, first_lane, sign):
        hit = (((prow == h) & (pcol == first_lane))
               | ((prow == h + FOX_HEADS) & (pcol == first_lane + 1))
               | ((prow == h + 2 * FOX_HEADS) & (pcol == first_lane + 2)))
        return jnp.where(hit, sign, 0.0).astype(BF16)

    def augment(x, c3, h, hh, gate_off, sign, ones_off):
        base = HEAD_DIM * (1 - hh)
        head_lanes = (lane >= HEAD_DIM * hh) & (lane < HEAD_DIM * (hh + 1))
        ones = (lane >= base + ones_off) & (lane < base + ones_off + 3)
        aug = jnp.dot(c3, place(h, base + gate_off, sign), preferred_element_type=F32)
        return jnp.where(head_lanes, x, jnp.where(ones, 1.0, aug).astype(BF16))

    for hh in range(2):
        for t in range(ntile):
            rows = slice(t * tq, (t + 1) * tq)
            kaug_ref[hh, t] = augment(k_ref[0, rows, :], c_ref[0, rows, :], 2 * j + hh, hh, 0, -1.0, 3)
            qaug_ref[hh, t] = augment(q_ref[0, rows, :], c_ref[0, rows, :], 2 * j + hh, hh, 3, 1.0, 0)

    _attn_sweep(vq_ref, vk_ref, vt_ref, o_ref, kaug_ref, qaug_ref, s_ref, m_ref, acc_ref)


def _fox_attn(qk3, cum3, vt4):
    b, s, _ = qk3.shape
    t = ATT_TILE
    npair = FOX_HEADS // 2
    in_specs = [pl.BlockSpec((1, s, LANES), lambda b_, j: (b_, 0, j)),
                pl.BlockSpec((1, s, LANES), lambda b_, j: (b_, 0, npair + j)),
                pl.BlockSpec((1, s, LANES), lambda b_, j: (b_, 0, 0)),
                pl.BlockSpec((1, s // t, LANES, t), lambda b_, j: (b_, 0, j, 0))]
    return _attn_call(_fox_attn_kernel, "fox_attn", b, s, npair, FOX_WIDTH,
                      (qk3, qk3, cum3, vt4), in_specs)


def _moba_attn_kernel(vq_ref, vk_ref, q_ref, k_ref, vt_ref, o_ref,
                      km_ref, kaug_ref, qaug_ref, s_ref, m_ref, acc_ref):
    ntile, tq = qaug_ref.shape[1], qaug_ref.shape[2]
    seq = k_ref.shape[1]
    nb = seq // MOBA_BLOCK
    nbp = km_ref.shape[1] // 3
    bpt = tq // MOBA_BLOCK
    blk_shift = MOBA_BLOCK.bit_length() - 1
    assert nbp <= HEAD_DIM
    lane = lax.broadcasted_iota(jnp.int32, (1, LANES), 1)

    rows = [jnp.sum(k_ref[0, n * MOBA_BLOCK:(n + 1) * MOBA_BLOCK, :].astype(F32),
                    axis=0, keepdims=True) * (1.0 / MOBA_BLOCK) for n in range(nb)]
    if nbp > nb:
        rows.append(jnp.zeros((nbp - nb, LANES), F32))
    kmean = jnp.concatenate(rows, axis=0)
    key_blk = lax.shift_right_logical(lax.broadcasted_iota(jnp.int32, (tq, LANES), 0), blk_shift)
    lane2 = lax.broadcasted_iota(jnp.int32, (tq, LANES), 1)
    nidx = lax.broadcasted_iota(jnp.int32, (nbp, tq), 0)
    qcol = lax.broadcasted_iota(jnp.int32, (nbp, tq), 1)
    for hh in range(2):
        head_lanes = (lane >= HEAD_DIM * hh) & (lane < HEAD_DIM * (hh + 1))
        base = HEAD_DIM * (1 - hh)
        km_ref[hh] = jnp.concatenate(_split3(jnp.where(head_lanes, kmean, 0.0)), axis=0)
        for t in range(ntile):
            onehot = jnp.where(lane2 - base == key_blk + t * bpt, 1.0, 0.0)
            kaug_ref[hh, t] = jnp.where(head_lanes, k_ref[0, t * tq:(t + 1) * tq, :],
                                        onehot.astype(BF16))
        for t in range(ntile):
            q = q_ref[0, t * tq:(t + 1) * tq, :]
            qblk = lax.shift_right_logical(t * tq + qcol, blk_shift)
            past = nidx < qblk
            qh = jnp.where(head_lanes, q, jnp.zeros_like(q))
            g3 = lax.dot_general(km_ref[hh], qh, _NT, preferred_element_type=F32)
            gate = g3[0:nbp] + g3[nbp:2 * nbp] + g3[2 * nbp:3 * nbp]
            g = jnp.where(past, gate, -jnp.inf)
            sel = nidx == qblk
            for _ in range(MOBA_TOPK):
                first = jnp.min(jnp.where(g == jnp.max(g, axis=0, keepdims=True), nidx, nbp),
                                axis=0, keepdims=True)
                pick = nidx == first
                sel = sel | (pick & past)
                g = jnp.where(pick, -jnp.inf, g)
            parts = [jnp.zeros((base, tq), F32), jnp.where(sel, 0.0, NEG_BIG),
                     jnp.zeros((LANES - base - nbp, tq), F32)]
            bias = jnp.concatenate([r for r in parts if r.shape[0]], axis=0)
            qaug_ref[hh, t] = jnp.where(head_lanes, q, bias.T.astype(BF16))

    _attn_sweep(vq_ref, vk_ref, vt_ref, o_ref, kaug_ref, qaug_ref, s_ref, m_ref, acc_ref)


def _moba_attn(qk3, vt4):
    b, s, _ = qk3.shape
    t = ATT_TILE
    npair = MOBA_HEADS // 2
    nbp = -(-(s // MOBA_BLOCK) // 16) * 16
    in_specs = [pl.BlockSpec((1, s, LANES), lambda b_, j: (b_, 0, j)),
                pl.BlockSpec((1, s, LANES), lambda b_, j: (b_, 0, npair + j)),
                pl.BlockSpec((1, s // t, LANES, t), lambda b_, j: (b_, 0, j, 0))]
    return _attn_call(_moba_attn_kernel, "moba_attn", b, s, npair, MOBA_WIDTH,
                      (qk3, qk3, vt4), in_specs,
                      extra_scratch=[pltpu.VMEM((2, 3 * nbp, LANES), BF16)])


def _pool_kernel(u_ref, w_ref, sc_ref, o_ref):
    g = pl.program_id(1)
    u = u_ref[0]
    t = lax.broadcasted_iota(jnp.int32, u.shape, 0)
    wsum = u
    acc = jnp.zeros_like(u)
    for r, w in enumerate(POOL_WINDOWS):
        d = w // 2
        wsum = wsum + jnp.where(t >= d, pltpu.roll(wsum, d, 0), 0.0)
        cnt = jnp.minimum(t + 1, w).astype(F32)
        acc = jnp.where(g == r, wsum / cnt, acc)
    mixed = (acc - u).astype(BF16)
    y = jnp.dot(mixed, w_ref[0], preferred_element_type=F32) * sc_ref[...]
    o_ref[0] = y.astype(o_ref.dtype)


def _pool(u3, pool_w, pool_scale):
    b, s, _ = u3.shape
    ng = len(POOL_WINDOWS)
    return pl.pallas_call(
        _pool_kernel,
        grid=(b, ng),
        in_specs=[pl.BlockSpec((1, s, LANES), lambda b_, g: (b_, 0, g)),
                  pl.BlockSpec((1, POOL_GROUP_DIM, POOL_GROUP_DIM), lambda b_, g: (g, 0, 0)),
                  pl.BlockSpec((1, LANES), lambda b_, g: (0, g))],
        out_specs=pl.BlockSpec((1, s, LANES), lambda b_, g: (b_, 0, g)),
        out_shape=jax.ShapeDtypeStruct((b, s, POOL_WIDTH), BF16),
        compiler_params=_params(("arbitrary", "arbitrary")),
        name="pool",
    )(u3, pool_w, pool_scale)


def _outproj_ln_kernel(*refs, n_in):
    a_refs = refs[:n_in]
    w_refs = refs[n_in:2 * n_in]
    x_ref, g_ref, b_ref, o_ref = refs[2 * n_in:]
    mix = jnp.dot(a_refs[0][...], w_refs[0][...], preferred_element_type=F32)
    for a_ref, w_ref in zip(a_refs[1:], w_refs[1:]):
        mix = mix + jnp.dot(a_ref[...], w_ref[...], preferred_element_type=F32)
    o_ref[...] = _ln(ALPHA * x_ref[...] + mix, g_ref[...], b_ref[...])


def _outproj_ln(acts, weights, x2, gamma, beta):
    n, d = x2.shape
    tm = ROW_TILE
    in_specs = [pl.BlockSpec((tm, a.shape[1]), lambda i: (i, 0)) for a in acts]
    in_specs += [pl.BlockSpec(w.shape, lambda i: (0, 0)) for w in weights]
    in_specs += [pl.BlockSpec((tm, d), lambda i: (i, 0)),
                 pl.BlockSpec((1, d), lambda i: (0, 0)),
                 pl.BlockSpec((1, d), lambda i: (0, 0))]
    return pl.pallas_call(
        functools.partial(_outproj_ln_kernel, n_in=len(acts)),
        grid=(n // tm,),
        in_specs=in_specs,
        out_specs=pl.BlockSpec((tm, d), lambda i: (i, 0)),
        out_shape=jax.ShapeDtypeStruct((n, d), F32),
        compiler_params=_params(("arbitrary",)),
        name="outproj_ln",
    )(*acts, *weights, x2, gamma, beta)


def _swiglu_chunk(xb, wg_ref, wu_ref, wd_ref):
    hid = (jax.nn.silu(jnp.dot(xb, wg_ref[...].astype(BF16), preferred_element_type=F32))
           * jnp.dot(xb, wu_ref[...].astype(BF16), preferred_element_type=F32))
    return jnp.dot(hid.astype(BF16), wd_ref[...].astype(BF16), preferred_element_type=F32)


def _ffn_kernel(x_ref, wg_ref, wu_ref, wd_ref, g_ref, b_ref, o_ref, acc_ref, xb_ref):
    f = pl.program_id(1)

    @pl.when(f == 0)
    def _():
        acc_ref[...] = jnp.zeros_like(acc_ref)
        xb_ref[...] = x_ref[...].astype(BF16)

    acc_ref[...] += _swiglu_chunk(xb_ref[...], wg_ref, wu_ref, wd_ref)

    @pl.when(f == pl.num_programs(1) - 1)
    def _():
        o_ref[...] = _ln(ALPHA * x_ref[...] + acc_ref[...], g_ref[...], b_ref[...])


def _ffn_ln(x2, wg, wu, wd, gamma, beta):
    n, d = x2.shape
    tm, tf = FFN_ROW_TILE, FFN_COL_TILE
    ff = wg.shape[1]
    return pl.pallas_call(
        _ffn_kernel,
        grid=(n // tm, ff // tf),
        in_specs=[pl.BlockSpec((tm, d), lambda i, f: (i, 0)),
                  pl.BlockSpec((d, tf), lambda i, f: (0, f)),
                  pl.BlockSpec((d, tf), lambda i, f: (0, f)),
                  pl.BlockSpec((tf, d), lambda i, f: (f, 0)),
                  pl.BlockSpec((1, d), lambda i, f: (0, 0)),
                  pl.BlockSpec((1, d), lambda i, f: (0, 0))],
        out_specs=pl.BlockSpec((tm, d), lambda i, f: (i, 0)),
        out_shape=jax.ShapeDtypeStruct((n, d), F32),
        scratch_shapes=[pltpu.VMEM((tm, d), F32), pltpu.VMEM((tm, d), BF16)],
        compiler_params=_params(("arbitrary", "arbitrary")),
        name="ffn_ln",
    )(x2, wg, wu, wd, gamma, beta)


def _router_kernel(x_ref, w_ref, info_ref, cnt_ref, seen_ref):
    @pl.when(pl.program_id(0) == 0)
    def _():
        seen_ref[...] = jnp.zeros_like(seen_ref)

    x = x_ref[...]
    tm = x.shape[0]
    xh = x.astype(BF16)
    xm = (x - xh.astype(F32)).astype(BF16)
    l3 = (jnp.dot(xh, w_ref[...], preferred_element_type=F32)
          + jnp.dot(xm, w_ref[...], preferred_element_type=F32))
    logits = l3[:, 0:LANES] + l3[:, LANES:2 * LANES] + l3[:, 2 * LANES:3 * LANES]
    lane = lax.broadcasted_iota(jnp.int32, logits.shape, 1)
    g = jnp.where(lane < N_EXPERTS, logits, -jnp.inf)
    v1 = jnp.max(g, axis=1, keepdims=True)
    i1 = jnp.min(jnp.where(g == v1, lane, LANES), axis=1, keepdims=True)
    p1 = lane == i1
    g = jnp.where(p1, -jnp.inf, g)
    v2 = jnp.max(g, axis=1, keepdims=True)
    i2 = jnp.min(jnp.where(g == v2, lane, LANES), axis=1, keepdims=True)
    p2 = lane == i2
    e2 = jnp.exp(v2 - v1)
    den = 1.0 + e2
    member = jnp.where(p1 | p2, 1.0, 0.0)
    earlier = (lax.broadcasted_iota(jnp.int32, (tm, tm), 1)
               < lax.broadcasted_iota(jnp.int32, (tm, tm), 0)).astype(BF16)
    before = jnp.dot(earlier, member.astype(BF16), preferred_element_type=F32) + seen_ref[...]
    r1 = jnp.sum(jnp.where(p1, before, 0.0), axis=1, keepdims=True)
    r2 = jnp.sum(jnp.where(p2, before, 0.0), axis=1, keepdims=True)
    seen_ref[...] += jnp.sum(member, axis=0, keepdims=True)
    cnt_ref[...] = seen_ref[...]
    cols = (i1.astype(F32), i2.astype(F32), r1, r2, 1.0 / den, e2 / den)
    info = jnp.zeros_like(logits)
    for c, v in enumerate(cols):
        info = jnp.where(lane == c, v, info)
    info_ref[...] = info


def _router(x2, w3):
    n, d = x2.shape
    tm = ROW_TILE
    return pl.pallas_call(
        _router_kernel,
        grid=(n // tm,),
        in_specs=[pl.BlockSpec((tm, d), lambda i: (i, 0)),
                  pl.BlockSpec(w3.shape, lambda i: (0, 0))],
        out_specs=[pl.BlockSpec((tm, LANES), lambda i: (i, 0)),
                   pl.BlockSpec((1, LANES), lambda i: (0, 0))],
        out_shape=[jax.ShapeDtypeStruct((n, LANES), F32),
                   jax.ShapeDtypeStruct((1, LANES), F32)],
        scratch_shapes=[pltpu.VMEM((1, LANES), F32)],
        compiler_params=_params(("arbitrary",)),
        name="router",
    )(x2, w3)


def _route_plan(info, counts, n_tiles):
    tmf = MOE_ROW_TILE
    cnt = counts[0, :N_EXPERTS].astype(jnp.int32)
    padded = (cnt + tmf - 1) // tmf * tmf
    ends = jnp.cumsum(padded)
    offs = ends - padded
    eid = info[:, 0:2].astype(jnp.int32)
    rank = info[:, 2:4].astype(jnp.int32)
    pos = rank + jnp.sum(jnp.where(eid[..., None] == jnp.arange(N_EXPERTS), offs, 0), axis=-1)
    n_used = ends[-1] // tmf
    tile = jnp.arange(n_tiles)
    owner = jnp.minimum(jnp.sum(tile[:, None] * tmf >= ends[None, :], axis=1), N_EXPERTS - 1)
    owner = jnp.where(tile < n_used, owner, owner[n_used - 1])
    return (pos.reshape(-1, 1, 2 * ROW_TILE), owner.astype(jnp.int32),
            n_used.reshape(1).astype(jnp.int32))


def _to_token_major(dst_ref, val):
    tm = val.shape[0]
    for s in range(ROW_SLABS):
        dst_ref[pl.ds(s, tm, stride=ROW_SLABS), :] = val[:, s * LANES:(s + 1) * LANES]


def _from_token_major(src_ref):
    tm = src_ref.shape[0] // ROW_SLABS
    return jnp.concatenate([src_ref[pl.ds(s, tm, stride=ROW_SLABS), :] for s in range(ROW_SLABS)],
                           axis=1)


def _token_rows(ref, t):
    return ref.at[pl.ds(pl.multiple_of(t * ROW_SLABS, ROW_SLABS), ROW_SLABS), :]


def _dispatch_kernel(pos_ref, x_ref, zero_ref, xs_ref, xt_ref, sems):
    del zero_ref
    tm = x_ref.shape[0]
    i = pl.program_id(0)
    slot = lax.rem(i, 2)
    _to_token_major(xt_ref.at[slot], x_ref[...])

    def row_copy(t, s):
        return pltpu.make_async_copy(_token_rows(xt_ref.at[slot], t),
                                     _token_rows(xs_ref, pos_ref[0, 2 * t + s]), sems.at[slot])

    def issue(t, c):
        row_copy(t, 0).start()
        row_copy(t, 1).start(priority=1)
        return c

    lax.fori_loop(0, tm, issue, 0, unroll=8)

    def drain(sl):
        for _ in range(2):
            pltpu.make_async_copy(xt_ref.at[sl], xs_ref.at[pl.ds(0, tm * ROW_SLABS), :],
                                  sems.at[sl]).wait()

    @pl.when(i > 0)
    def _():
        drain(1 - slot)

    @pl.when(i == pl.num_programs(0) - 1)
    def _():
        drain(slot)


def _dispatch(x2, pos, n_rows):
    n, d = x2.shape
    tm = ROW_TILE
    return pl.pallas_call(
        _dispatch_kernel,
        grid=(n // tm,),
        in_specs=[pl.BlockSpec((None, 1, 2 * tm), lambda i: (i, 0, 0), memory_space=pltpu.SMEM),
                  pl.BlockSpec((tm, d), lambda i: (i, 0)),
                  pl.BlockSpec(memory_space=pl.ANY)],
        out_specs=pl.BlockSpec(memory_space=pl.ANY),
        out_shape=jax.ShapeDtypeStruct((n_rows * ROW_SLABS, LANES), F32),
        input_output_aliases={2: 0},
        scratch_shapes=[pltpu.VMEM((2, tm * ROW_SLABS, LANES), F32), pltpu.SemaphoreType.DMA((2,))],
        compiler_params=_params(("arbitrary",)),
        name="moe_dispatch",
    )(pos, x2, jnp.zeros((n_rows * ROW_SLABS, LANES), F32))


def _moe_ffn_kernel(owner_ref, used_ref, xs_ref, wg_ref, wu_ref, wd_ref, ys_ref, acc_ref, xb_ref):
    del owner_ref
    i, f = pl.program_id(0), pl.program_id(1)
    used = i < used_ref[0]

    @pl.when(f == 0)
    def _():
        acc_ref[...] = jnp.zeros_like(acc_ref)
        xb_ref[...] = _from_token_major(xs_ref).astype(BF16)

    @pl.when(used)
    def _():
        acc_ref[...] += _swiglu_chunk(xb_ref[...], wg_ref, wu_ref, wd_ref)

    @pl.when(f == pl.num_programs(1) - 1)
    def _():
        _to_token_major(ys_ref, acc_ref[...])


def _moe_ffn(xs, owner, n_used, layer, wg, wu, wd):
    d = wg.shape[2]
    n_rows = xs.shape[0] // ROW_SLABS
    tm, tf = MOE_ROW_TILE, FFN_COL_TILE
    ff = wg.shape[3]
    nf = ff // tf
    col = lambda i, f, used: jnp.where(i < used[0], f, nf - 1)
    return pl.pallas_call(
        _moe_ffn_kernel,
        grid_spec=pltpu.PrefetchScalarGridSpec(
            num_scalar_prefetch=2,
            grid=(n_rows // tm, nf),
            in_specs=[
                pl.BlockSpec((tm * ROW_SLABS, LANES),
                             lambda i, f, own, used: (jnp.minimum(i, used[0] - 1), 0)),
                pl.BlockSpec((None, None, d, tf),
                             lambda i, f, own, used: (layer, own[i], 0, col(i, f, used))),
                pl.BlockSpec((None, None, d, tf),
                             lambda i, f, own, used: (layer, own[i], 0, col(i, f, used))),
                pl.BlockSpec((None, None, tf, d),
                             lambda i, f, own, used: (layer, own[i], col(i, f, used), 0)),
            ],
            out_specs=pl.BlockSpec((tm * ROW_SLABS, LANES), lambda i, f, own, used: (i, 0)),
            scratch_shapes=[pltpu.VMEM((tm, d), F32), pltpu.VMEM((tm, d), BF16)],
        ),
        out_shape=jax.ShapeDtypeStruct((n_rows * ROW_SLABS, LANES), F32),
        compiler_params=_params(("arbitrary", "arbitrary")),
        name="moe_ffn",
    )(owner, n_used, xs, wg, wu, wd)


def _combine_ln_kernel(pos_ref, pos_next_ref, x_ref, info_ref, ys_ref, g_ref, b_ref, o_ref,
                       ybuf_ref, sems):
    tm = x_ref.shape[0]
    i = pl.program_id(0)
    slot = lax.rem(i, 2)

    def gather(p_ref, sl):
        def row_copy(t, s):
            return pltpu.make_async_copy(_token_rows(ys_ref, p_ref[0, 2 * t + s]),
                                         _token_rows(ybuf_ref.at[sl, s], t), sems.at[sl])

        def issue(t, c):
            row_copy(t, 0).start()
            row_copy(t, 1).start(priority=1)
            return c

        lax.fori_loop(0, tm, issue, 0, unroll=8)

    @pl.when(i == 0)
    def _():
        gather(pos_ref, 0)

    @pl.when(i + 1 < pl.num_programs(0))
    def _():
        gather(pos_next_ref, 1 - slot)

    for s in range(2):
        pltpu.make_async_copy(ys_ref.at[pl.ds(0, tm * ROW_SLABS), :], ybuf_ref.at[slot, s],
                              sems.at[slot]).wait()
    info = info_ref[...]
    lane = lax.broadcasted_iota(jnp.int32, info.shape, 1)
    w1 = jnp.sum(jnp.where(lane == 4, info, 0.0), axis=1, keepdims=True)
    w2 = jnp.sum(jnp.where(lane == 5, info, 0.0), axis=1, keepdims=True)
    y = (w1 * _from_token_major(ybuf_ref.at[slot, 0]) + w2 * _from_token_major(ybuf_ref.at[slot, 1]))
    o_ref[...] = _ln(ALPHA * x_ref[...] + y, g_ref[...], b_ref[...])


def _combine_ln(x2, info, pos, ys, gamma, beta):
    n, d = x2.shape
    tm = ROW_TILE
    last = n // tm - 1
    return pl.pallas_call(
        _combine_ln_kernel,
        grid=(n // tm,),
        in_specs=[pl.BlockSpec((None, 1, 2 * tm), lambda i: (i, 0, 0), memory_space=pltpu.SMEM),
                  pl.BlockSpec((None, 1, 2 * tm), lambda i: (jnp.minimum(i + 1, last), 0, 0),
                               memory_space=pltpu.SMEM),
                  pl.BlockSpec((tm, d), lambda i: (i, 0)),
                  pl.BlockSpec((tm, LANES), lambda i: (i, 0)),
                  pl.BlockSpec(memory_space=pl.ANY),
                  pl.BlockSpec((1, d), lambda i: (0, 0)),
                  pl.BlockSpec((1, d), lambda i: (0, 0))],
        out_specs=pl.BlockSpec((tm, d), lambda i: (i, 0)),
        out_shape=jax.ShapeDtypeStruct((n, d), F32),
        scratch_shapes=[pltpu.VMEM((2, 2, tm * ROW_SLABS, LANES), F32),
                        pltpu.SemaphoreType.DMA((2,))],
        compiler_params=_params(("arbitrary",)),
        name="moe_combine_ln",
    )(pos, pos, x2, info, ys, gamma, beta)


def _moe_ln(x2, w_router3, layer, wg, wu, wd, gamma, beta):
    n = x2.shape[0]
    n_tiles = 2 * n // MOE_ROW_TILE + N_EXPERTS
    info, counts = _router(x2, w_router3)
    pos, owner, n_used = _route_plan(info, counts, n_tiles)
    xs = _dispatch(x2, pos, n_tiles * MOE_ROW_TILE)
    ys = _moe_ffn(xs, owner, n_used, layer, wg, wu, wd)
    return _combine_ln(x2, info, pos, ys, gamma, beta)


def _pad_cols(w, width):
    return jnp.pad(w, ((0, 0), (0, width - w.shape[1])))


def _even_weights(w_in, b_forget):
    wq, wk, wv, wf, wu = jnp.split(
        w_in, [FOX_WIDTH, 2 * FOX_WIDTH, 3 * FOX_WIDTH, 3 * FOX_WIDTH + FOX_HEADS], axis=1)
    w_qk = jnp.concatenate([wq, wk], axis=1).astype(BF16)
    w_vt = wv.T.astype(BF16)
    w_f3 = _pad_cols(jnp.concatenate([wf, wf, wf], axis=1), LANES).astype(BF16)
    b_f3 = _pad_cols(jnp.concatenate([b_forget, b_forget, b_forget])[None, :], LANES)
    return w_qk, w_vt, w_f3, b_f3, wu.astype(BF16)


def _odd_weights(w_in):
    wq, wk, wv = jnp.split(w_in, 3, axis=1)
    return jnp.concatenate([wq, wk], axis=1).astype(BF16), wv.T.astype(BF16)


def _router_weights(w_router):
    return jnp.concatenate([_pad_cols(p.astype(F32), LANES).astype(BF16)
                            for p in _split3(w_router)], axis=1)


def kernel(x, ln1_g, ln1_b, ln2_g, ln2_b, ev_w_in, ev_b_forget, ev_pool_w, ev_pool_scale, ev_w_out,
           ev_ffn_gate, ev_ffn_up, ev_ffn_down, od_w_in, od_w_out, od_router, od_exp_gate, od_exp_up,
           od_exp_down):
    bsz, seq, d = x.shape
    n = bsz * seq
    x2 = x.reshape(n, d)
    for layer in range(DEPTH):
        li = layer // 2
        g1, b1 = ln1_g[layer][None, :], ln1_b[layer][None, :]
        g2, b2 = ln2_g[layer][None, :], ln2_b[layer][None, :]
        if layer % 2 == 0:
            w_qk, w_vt, w_f3, b_f3, w_u = _even_weights(ev_w_in[li], ev_b_forget[li])
            qk, vt, z, u = _proj(x2, seq, [(w_qk, "qk", BF16), (w_vt, "nt", BF16),
                                           (w_f3, "nn", F32), (w_u, "nn", F32)])
            cum3 = _fox_cum(z.reshape(bsz, seq, LANES), b_f3)
            att = _fox_attn(qk.reshape(bsz, seq, -1), cum3, vt)
            pooled = _pool(u.reshape(bsz, seq, POOL_WIDTH), ev_pool_w[li].astype(BF16),
                           ev_pool_scale[li][None, :])
            w_out = ev_w_out[li].astype(BF16)
            x2 = _outproj_ln([att.reshape(n, FOX_WIDTH), pooled.reshape(n, POOL_WIDTH)],
                             [w_out[:FOX_WIDTH], w_out[FOX_WIDTH:]], x2, g1, b1)
            x2 = _ffn_ln(x2, ev_ffn_gate[li].astype(BF16), ev_ffn_up[li].astype(BF16),
                         ev_ffn_down[li].astype(BF16), g2, b2)
        else:
            w_qk, w_vt = _odd_weights(od_w_in[li])
            qk, vt = _proj(x2, seq, [(w_qk, "qk", BF16), (w_vt, "nt", BF16)])
            att = _moba_attn(qk.reshape(bsz, seq, -1), vt)
            x2 = _outproj_ln([att.reshape(n, MOBA_WIDTH)], [od_w_out[li].astype(BF16)], x2, g1, b1)
            x2 = _moe_ln(x2, _router_weights(od_router[li]), li, od_exp_gate, od_exp_up, od_exp_down,
                         g2, b2)
    return x2.reshape(bsz, seq, d)
```

```python
import functools

import jax
import jax.numpy as jnp
from jax import lax
from jax.experimental import pallas as pl
from jax.experimental.pallas import tpu as pltpu

F32 = jnp.float32
BF16 = jnp.bfloat16

D_MODEL = 1024
DEPTH = 4
HEAD_DIM = 64
FOX_HEADS = 8
FOX_WIDTH = FOX_HEADS * HEAD_DIM
POOL_WINDOWS = (2, 4, 8, 16)
POOL_WIDTH = 512
POOL_GROUP_DIM = 128
MOBA_HEADS = 16
MOBA_WIDTH = MOBA_HEADS * HEAD_DIM
MOBA_BLOCK = 256
MOBA_TOPK = 3
D_FF = 3584
N_EXPERTS = 8
LN_EPS = 1e-5
ALPHA = (2 * DEPTH) ** 0.25
LOG2E = 1.4426950408889634
Q_LOG2_SCALE = HEAD_DIM ** -0.5 * LOG2E

LANES = 128
ROW_SLABS = D_MODEL // LANES
ATT_TILE = 512
ROW_TILE = 512
FFN_ROW_TILE = 1024
FFN_COL_TILE = 512
MOE_ROW_TILE = 1024
CUM_BLOCK = 256
NEG_BIG = -1e30
VMEM_LIMIT = 48 * 1024 * 1024

_NT = (((1,), (1,)), ((), ()))


def _params(sem):
    return pltpu.CompilerParams(dimension_semantics=sem, vmem_limit_bytes=VMEM_LIMIT)


def _split3(x):
    hi = x.astype(BF16)
    r = x - hi.astype(F32)
    mid = r.astype(BF16)
    lo = (r - mid.astype(F32)).astype(BF16)
    return hi, mid, lo


def _ln(y, g, b):
    mu = jnp.mean(y, axis=-1, keepdims=True)
    yc = y - mu
    var = jnp.mean(yc * yc, axis=-1, keepdims=True)
    return yc * lax.rsqrt(var + LN_EPS) * g + b


def _proj_kernel(*refs, kinds):
    x_ref = refs[0]
    w_refs = refs[1:1 + len(kinds)]
    o_refs = refs[1 + len(kinds):]
    xb = x_ref[...].astype(BF16)
    for kind, w_ref, o_ref in zip(kinds, w_refs, o_refs):
        if kind == "nt":
            m = w_ref.shape[0]
            for c0 in range(0, m, 512):
                c1 = min(c0 + 512, m)
                o_ref[0, 0, c0:c1, :] = lax.dot_general(
                    w_ref[c0:c1, :], xb, _NT, preferred_element_type=F32).astype(o_ref.dtype)
        else:
            m = w_ref.shape[1]
            for c0 in range(0, m, 512):
                c1 = min(c0 + 512, m)
                y = jnp.dot(xb, w_ref[:, c0:c1], preferred_element_type=F32)
                if kind == "qk" and c0 < m // 2:
                    y = y * Q_LOG2_SCALE
                o_ref[:, c0:c1] = y.astype(o_ref.dtype)


def _proj(x2, seq, specs):
    n, d = x2.shape
    tm = ROW_TILE
    tps = seq // tm
    kinds = tuple(k for _, k, _ in specs)
    in_specs = [pl.BlockSpec((tm, d), lambda i: (i, 0))]
    out_specs, out_shapes = [], []
    for w, kind, dt in specs:
        in_specs.append(pl.BlockSpec(w.shape, lambda i: (0, 0)))
        if kind == "nt":
            m = w.shape[0]
            out_shapes.append(jax.ShapeDtypeStruct((n // seq, tps, m, tm), dt))
            out_specs.append(pl.BlockSpec((1, 1, m, tm), lambda i: (i // tps, i % tps, 0, 0)))
        else:
            m = w.shape[1]
            out_shapes.append(jax.ShapeDtypeStruct((n, m), dt))
            out_specs.append(pl.BlockSpec((tm, m), lambda i: (i, 0)))
    return pl.pallas_call(
        functools.partial(_proj_kernel, kinds=kinds),
        grid=(n // tm,),
        in_specs=in_specs,
        out_specs=out_specs,
        out_shape=out_shapes,
        compiler_params=_params(("arbitrary",)),
        name="proj",
    )(x2, *[w for w, _, _ in specs])


def _fox_cum_kernel(z_ref, b_ref, o_ref):
    seq = z_ref.shape[1]
    blk = CUM_BLOCK
    row = lax.broadcasted_iota(jnp.int32, (blk, blk), 0)
    col = lax.broadcasted_iota(jnp.int32, (blk, blk), 1)
    tri = (col <= row).astype(BF16)
    lane = lax.broadcasted_iota(jnp.int32, (blk, LANES), 1)
    carry = jnp.zeros((1, LANES), F32)
    for t in range(seq // blk):
        logf = jax.nn.log_sigmoid(z_ref[0, t * blk:(t + 1) * blk, :] + b_ref[...])
        hi, mid, lo = _split3(logf)
        c = (jnp.dot(tri, hi, preferred_element_type=F32)
             + jnp.dot(tri, mid, preferred_element_type=F32)
             + jnp.dot(tri, lo, preferred_element_type=F32)) + carry
        carry = c[blk - 1:blk, :]
        chi, cmid, clo = _split3(c * LOG2E)
        o_ref[0, t * blk:(t + 1) * blk, :] = jnp.where(
            lane < FOX_HEADS, chi, jnp.where(lane < 2 * FOX_HEADS, cmid, clo))


def _fox_cum(z3, bias3):
    b, s, _ = z3.shape
    return pl.pallas_call(
        _fox_cum_kernel,
        grid=(b,),
        in_specs=[pl.BlockSpec((1, s, LANES), lambda i: (i, 0, 0)),
                  pl.BlockSpec((1, LANES), lambda i: (0, 0))],
        out_specs=pl.BlockSpec((1, s, LANES), lambda i: (i, 0, 0)),
        out_shape=jax.ShapeDtypeStruct((b, s, LANES), BF16),
        compiler_params=_params(("arbitrary",)),
        name="fox_cum",
    )(z3, bias3)


SUM_ROWS = 16
PIPE_UNROLL = 14


def _softmax_step(s, vt1, m, acc):
    m_new = jnp.maximum(m, jnp.max(s, axis=0, keepdims=True))
    p = jnp.exp2(s - m_new).astype(BF16)
    acc_new = jnp.exp2(m - m_new) * acc + jnp.dot(vt1, p, preferred_element_type=F32)
    return m_new, acc_new


def _attn_scratch(s, t):
    nt = s // t
    return [pltpu.VMEM((2, nt, t, LANES), BF16), pltpu.VMEM((2, nt, t, LANES), BF16),
            pltpu.VMEM((2, 2, t, t), F32), pltpu.VMEM((nt, 2, 1, t), F32),
            pltpu.VMEM((nt, 2, HEAD_DIM + SUM_ROWS, t), F32)]


def _visit_tables(ntile):
    pairs = [(q, q - d) for d in range(1, ntile) for q in range(d, ntile)]
    if not pairs:
        pairs = [(0, 0)]
    return (jnp.asarray([q for q, _ in pairs], jnp.int32), jnp.asarray([k for _, k in pairs], jnp.int32))


def _attn_sweep(vq_ref, vk_ref, vt_ref, o_ref, kaug_ref, qaug_ref, s_ref, m_ref, acc_ref):
    ntile, tq = qaug_ref.shape[1], qaug_ref.shape[2]
    nvis = ntile * (ntile - 1) // 2
    causal = (lax.broadcasted_iota(jnp.int32, (tq, tq), 0)
              <= lax.broadcasted_iota(jnp.int32, (tq, tq), 1))
    ones = jnp.ones((SUM_ROWS, tq), BF16)

    def produce(buf, qi, ki, diagonal=False):
        for hh in range(2):
            s = lax.dot_general(kaug_ref[hh, ki], qaug_ref[hh, qi], _NT, preferred_element_type=F32)
            s_ref[buf, hh] = jnp.where(causal, s, -jnp.inf) if diagonal else s

    def consume(buf, qi, ki):
        for hh in range(2):
            vt1 = jnp.concatenate([vt_ref[0, ki, HEAD_DIM * hh:HEAD_DIM * (hh + 1), :], ones], axis=0)
            m_ref[qi, hh], acc_ref[qi, hh] = _softmax_step(
                s_ref[buf, hh], vt1, m_ref[qi, hh], acc_ref[qi, hh])

    def pipeline(count, visit, diagonal):
        last = count - 1
        produce(0, *visit(0), diagonal)

        def body(a, c):
            for r in range(PIPE_UNROLL):
                v = PIPE_UNROLL * a + r
                produce((r + 1) % 2, *visit(jnp.minimum(v + 1, last)), diagonal)
                consume(r % 2, *visit(v))
            return c

        lax.fori_loop(0, count // PIPE_UNROLL, body, 0)
        for v in range(count - count % PIPE_UNROLL, count):
            if v < last:
                produce((v + 1) % 2, *visit(v + 1), diagonal)
            consume(v % 2, *visit(v))

    m_ref[...] = jnp.full(m_ref.shape, -jnp.inf, F32)
    acc_ref[...] = jnp.zeros(acc_ref.shape, F32)
    pipeline(ntile, lambda v: (v, v), True)
    if nvis:
        pipeline(nvis, lambda v: (vq_ref[v], vk_ref[v]), False)
    for qi in range(ntile):
        o = jnp.concatenate([acc_ref[qi, hh, 0:HEAD_DIM] / acc_ref[qi, hh, HEAD_DIM:HEAD_DIM + 1]
                             for hh in range(2)], axis=0)
        o_ref[0, qi * tq:(qi + 1) * tq, :] = o.T.astype(o_ref.dtype)


def _attn_call(kernel_fn, name, b, s, npair, width, operands, in_specs, extra_scratch=()):
    t = ATT_TILE
    vq, vk = _visit_tables(s // t)
    smem = pl.BlockSpec(memory_space=pltpu.SMEM)
    return pl.pallas_call(
        kernel_fn,
        grid=(b, npair),
        in_specs=[smem, smem] + in_specs,
        out_specs=pl.BlockSpec((1, s, LANES), lambda b_, j: (b_, 0, j)),
        out_shape=jax.ShapeDtypeStruct((b, s, width), BF16),
        scratch_shapes=list(extra_scratch) + _attn_scratch(s, t),
        compiler_params=_params(("arbitrary", "arbitrary")),
        name=name,
    )(vq, vk, *operands)


def _fox_attn_kernel(vq_ref, vk_ref, q_ref, k_ref, c_ref, vt_ref, o_ref,
                     kaug_ref, qaug_ref, s_ref, m_ref, acc_ref):
    ntile, tq = qaug_ref.shape[1], qaug_ref.shape[2]
    j = pl.program_id(1)
    lane = lax.broadcasted_iota(jnp.int32, (1, LANES), 1)
    prow = lax.broadcasted_iota(jnp.int32, (LANES, LANES), 0)
    pcol = lax.broadcasted_iota(jnp.int32, (LANES, LANES), 1)

    def place(h, first_lane, sign):
        hit = (((prow == h) & (pcol == first_lane))
               | ((prow == h + FOX_HEADS) & (pcol == first_lane + 1))
               | ((prow == h + 2 * FOX_HEADS) & (pcol == first_lane + 2)))
        return jnp.where(hit, sign, 0.0).astype(BF16)

    def augment(x, c3, h, hh, gate_off, sign, ones_off):
        base = HEAD_DIM * (1 - hh)
        head_lanes = (lane >= HEAD_DIM * hh) & (lane < HEAD_DIM * (hh + 1))
        ones = (lane >= base + ones_off) & (lane < base + ones_off + 3)
        aug = jnp.dot(c3, place(h, base + gate_off, sign), preferred_element_type=F32)
        return jnp.where(head_lanes, x, jnp.where(ones, 1.0, aug).astype(BF16))

    for hh in range(2):
        for t in range(ntile):
            rows = slice(t * tq, (t + 1) * tq)
            kaug_ref[hh, t] = augment(k_ref[0, rows, :], c_ref[0, rows, :], 2 * j + hh, hh, 0, -1.0, 3)
            qaug_ref[hh, t] = augment(q_ref[0, rows, :], c_ref[0, rows, :], 2 * j + hh, hh, 3, 1.0, 0)

    _attn_sweep(vq_ref, vk_ref, vt_ref, o_ref, kaug_ref, qaug_ref, s_ref, m_ref, acc_ref)


def _fox_attn(qk3, cum3, vt4):
    b, s, _ = qk3.shape
    t = ATT_TILE
    npair = FOX_HEADS // 2
    in_specs = [pl.BlockSpec((1, s, LANES), lambda b_, j: (b_, 0, j)),
                pl.BlockSpec((1, s, LANES), lambda b_, j: (b_, 0, npair + j)),
                pl.BlockSpec((1, s, LANES), lambda b_, j: (b_, 0, 0)),
                pl.BlockSpec((1, s // t, LANES, t), lambda b_, j: (b_, 0, j, 0))]
    return _attn_call(_fox_attn_kernel, "fox_attn", b, s, npair, FOX_WIDTH,
                      (qk3, qk3, cum3, vt4), in_specs)


def _moba_attn_kernel(vq_ref, vk_ref, q_ref, k_ref, vt_ref, o_ref,
                      km_ref, kaug_ref, qaug_ref, s_ref, m_ref, acc_ref):
    ntile, tq = qaug_ref.shape[1], qaug_ref.shape[2]
    seq = k_ref.shape[1]
    nb = seq // MOBA_BLOCK
    nbp = km_ref.shape[1] // 3
    bpt = tq // MOBA_BLOCK
    blk_shift = MOBA_BLOCK.bit_length() - 1
    assert nbp <= HEAD_DIM
    lane = lax.broadcasted_iota(jnp.int32, (1, LANES), 1)

    rows = [jnp.sum(k_ref[0, n * MOBA_BLOCK:(n + 1) * MOBA_BLOCK, :].astype(F32),
                    axis=0, keepdims=True) * (1.0 / MOBA_BLOCK) for n in range(nb)]
    if nbp > nb:
        rows.append(jnp.zeros((nbp - nb, LANES), F32))
    kmean = jnp.concatenate(rows, axis=0)
    key_blk = lax.shift_right_logical(lax.broadcasted_iota(jnp.int32, (tq, LANES), 0), blk_shift)
    lane2 = lax.broadcasted_iota(jnp.int32, (tq, LANES), 1)
    nidx = lax.broadcasted_iota(jnp.int32, (nbp, tq), 0)
    qcol = lax.broadcasted_iota(jnp.int32, (nbp, tq), 1)
    for hh in range(2):
        head_lanes = (lane >= HEAD_DIM * hh) & (lane < HEAD_DIM * (hh + 1))
        base = HEAD_DIM * (1 - hh)
        km_ref[hh] = jnp.concatenate(_split3(jnp.where(head_lanes, kmean, 0.0)), axis=0)
        for t in range(ntile):
            onehot = jnp.where(lane2 - base == key_blk + t * bpt, 1.0, 0.0)
            kaug_ref[hh, t] = jnp.where(head_lanes, k_ref[0, t * tq:(t + 1) * tq, :],
                                        onehot.astype(BF16))
        for t in range(ntile):
            q = q_ref[0, t * tq:(t + 1) * tq, :]
            qblk = lax.shift_right_logical(t * tq + qcol, blk_shift)
            past = nidx < qblk
            qh = jnp.where(head_lanes, q, jnp.zeros_like(q))
            g3 = lax.dot_general(km_ref[hh], qh, _NT, preferred_element_type=F32)
            gate = g3[0:nbp] + g3[nbp:2 * nbp] + g3[2 * nbp:3 * nbp]
            g = jnp.where(past, gate, -jnp.inf)
            sel = nidx == qblk
            for _ in range(MOBA_TOPK):
                first = jnp.min(jnp.where(g == jnp.max(g, axis=0, keepdims=True), nidx, nbp),
                                axis=0, keepdims=True)
                pick = nidx == first
                sel = sel | (pick & past)
                g = jnp.where(pick, -jnp.inf, g)
            parts = [jnp.zeros((base, tq), F32), jnp.where(sel, 0.0, NEG_BIG),
                     jnp.zeros((LANES - base - nbp, tq), F32)]
            bias = jnp.concatenate([r for r in parts if r.shape[0]], axis=0)
            qaug_ref[hh, t] = jnp.where(head_lanes, q, bias.T.astype(BF16))

    _attn_sweep(vq_ref, vk_ref, vt_ref, o_ref, kaug_ref, qaug_ref, s_ref, m_ref, acc_ref)


def _moba_attn(qk3, vt4):
    b, s, _ = qk3.shape
    t = ATT_TILE
    npair = MOBA_HEADS // 2
    nbp = -(-(s // MOBA_BLOCK) // 16) * 16
    in_specs = [pl.BlockSpec((1, s, LANES), lambda b_, j: (b_, 0, j)),
                pl.BlockSpec((1, s, LANES), lambda b_, j: (b_, 0, npair + j)),
                pl.BlockSpec((1, s // t, LANES, t), lambda b_, j: (b_, 0, j, 0))]
    return _attn_call(_moba_attn_kernel, "moba_attn", b, s, npair, MOBA_WIDTH,
                      (qk3, qk3, vt4), in_specs,
                      extra_scratch=[pltpu.VMEM((2, 3 * nbp, LANES), BF16)])


def _pool_kernel(u_ref, w_ref, sc_ref, o_ref):
    g = pl.program_id(1)
    u = u_ref[0]
    t = lax.broadcasted_iota(jnp.int32, u.shape, 0)
    wsum = u
    acc = jnp.zeros_like(u)
    for r, w in enumerate(POOL_WINDOWS):
        d = w // 2
        wsum = wsum + jnp.where(t >= d, pltpu.roll(wsum, d, 0), 0.0)
        cnt = jnp.minimum(t + 1, w).astype(F32)
        acc = jnp.where(g == r, wsum / cnt, acc)
    mixed = (acc - u).astype(BF16)
    y = jnp.dot(mixed, w_ref[0], preferred_element_type=F32) * sc_ref[...]
    o_ref[0] = y.astype(o_ref.dtype)


def _pool(u3, pool_w, pool_scale):
    b, s, _ = u3.shape
    ng = len(POOL_WINDOWS)
    return pl.pallas_call(
        _pool_kernel,
        grid=(b, ng),
        in_specs=[pl.BlockSpec((1, s, LANES), lambda b_, g: (b_, 0, g)),
                  pl.BlockSpec((1, POOL_GROUP_DIM, POOL_GROUP_DIM), lambda b_, g: (g, 0, 0)),
                  pl.BlockSpec((1, LANES), lambda b_, g: (0, g))],
        out_specs=pl.BlockSpec((1, s, LANES), lambda b_, g: (b_, 0, g)),
        out_shape=jax.ShapeDtypeStruct((b, s, POOL_WIDTH), BF16),
        compiler_params=_params(("arbitrary", "arbitrary")),
        name="pool",
    )(u3, pool_w, pool_scale)


def _outproj_ln_kernel(*refs, n_in):
    a_refs = refs[:n_in]
    w_refs = refs[n_in:2 * n_in]
    x_ref, g_ref, b_ref, o_ref = refs[2 * n_in:]
    mix = jnp.dot(a_refs[0][...], w_refs[0][...], preferred_element_type=F32)
    for a_ref, w_ref in zip(a_refs[1:], w_refs[1:]):
        mix = mix + jnp.dot(a_ref[...], w_ref[...], preferred_element_type=F32)
    o_ref[...] = _ln(ALPHA * x_ref[...] + mix, g_ref[...], b_ref[...])


def _outproj_ln(acts, weights, x2, gamma, beta):
    n, d = x2.shape
    tm = ROW_TILE
    in_specs = [pl.BlockSpec((tm, a.shape[1]), lambda i: (i, 0)) for a in acts]
    in_specs += [pl.BlockSpec(w.shape, lambda i: (0, 0)) for w in weights]
    in_specs += [pl.BlockSpec((tm, d), lambda i: (i, 0)),
                 pl.BlockSpec((1, d), lambda i: (0, 0)),
                 pl.BlockSpec((1, d), lambda i: (0, 0))]
    return pl.pallas_call(
        functools.partial(_outproj_ln_kernel, n_in=len(acts)),
        grid=(n // tm,),
        in_specs=in_specs,
        out_specs=pl.BlockSpec((tm, d), lambda i: (i, 0)),
        out_shape=jax.ShapeDtypeStruct((n, d), F32),
        compiler_params=_params(("arbitrary",)),
        name="outproj_ln",
    )(*acts, *weights, x2, gamma, beta)


def _swiglu_chunk(xb, wg_ref, wu_ref, wd_ref):
    hid = (jax.nn.silu(jnp.dot(xb, wg_ref[...].astype(BF16), preferred_element_type=F32))
           * jnp.dot(xb, wu_ref[...].astype(BF16), preferred_element_type=F32))
    return jnp.dot(hid.astype(BF16), wd_ref[...].astype(BF16), preferred_element_type=F32)


def _ffn_kernel(x_ref, wg_ref, wu_ref, wd_ref, g_ref, b_ref, o_ref, acc_ref, xb_ref):
    f = pl.program_id(1)

    @pl.when(f == 0)
    def _():
        acc_ref[...] = jnp.zeros_like(acc_ref)
        xb_ref[...] = x_ref[...].astype(BF16)

    acc_ref[...] += _swiglu_chunk(xb_ref[...], wg_ref, wu_ref, wd_ref)

    @pl.when(f == pl.num_programs(1) - 1)
    def _():
        o_ref[...] = _ln(ALPHA * x_ref[...] + acc_ref[...], g_ref[...], b_ref[...])


def _ffn_ln(x2, wg, wu, wd, gamma, beta):
    n, d = x2.shape
    tm, tf = FFN_ROW_TILE, FFN_COL_TILE
    ff = wg.shape[1]
    return pl.pallas_call(
        _ffn_kernel,
        grid=(n // tm, ff // tf),
        in_specs=[pl.BlockSpec((tm, d), lambda i, f: (i, 0)),
                  pl.BlockSpec((d, tf), lambda i, f: (0, f)),
                  pl.BlockSpec((d, tf), lambda i, f: (0, f)),
                  pl.BlockSpec((tf, d), lambda i, f: (f, 0)),
                  pl.BlockSpec((1, d), lambda i, f: (0, 0)),
                  pl.BlockSpec((1, d), lambda i, f: (0, 0))],
        out_specs=pl.BlockSpec((tm, d), lambda i, f: (i, 0)),
        out_shape=jax.ShapeDtypeStruct((n, d), F32),
        scratch_shapes=[pltpu.VMEM((tm, d), F32), pltpu.VMEM((tm, d), BF16)],
        compiler_params=_params(("arbitrary", "arbitrary")),
        name="ffn_ln",
    )(x2, wg, wu, wd, gamma, beta)


EXPERT_ROWS = 16


def _router_kernel(x_ref, w_ref, info_ref, cnt_ref, seen_ref):
    @pl.when(pl.program_id(0) == 0)
    def _():
        seen_ref[...] = jnp.zeros_like(seen_ref)

    x = x_ref[...]
    tm = x.shape[0]
    ne = EXPERT_ROWS
    xh = x.astype(BF16)
    xm = (x - xh.astype(F32)).astype(BF16)
    l3 = (lax.dot_general(w_ref[...], xh, _NT, preferred_element_type=F32)
          + lax.dot_general(w_ref[...], xm, _NT, preferred_element_type=F32))
    logits = l3[0:ne] + l3[ne:2 * ne] + l3[2 * ne:3 * ne]
    eidx = lax.broadcasted_iota(jnp.int32, logits.shape, 0)
    g = jnp.where(eidx < N_EXPERTS, logits, -jnp.inf)
    v1 = jnp.max(g, axis=0, keepdims=True)
    i1 = jnp.min(jnp.where(g == v1, eidx, ne), axis=0, keepdims=True)
    p1 = eidx == i1
    g = jnp.where(p1, -jnp.inf, g)
    v2 = jnp.max(g, axis=0, keepdims=True)
    i2 = jnp.min(jnp.where(g == v2, eidx, ne), axis=0, keepdims=True)
    p2 = eidx == i2
    e2 = jnp.exp(v2 - v1)
    den = 1.0 + e2
    member = jnp.where(p1 | p2, 1.0, 0.0)
    earlier = (lax.broadcasted_iota(jnp.int32, (tm, tm), 0)
               < lax.broadcasted_iota(jnp.int32, (tm, tm), 1)).astype(BF16)
    before = jnp.dot(member.astype(BF16), earlier, preferred_element_type=F32) + seen_ref[:, 0:1]
    r1 = jnp.sum(jnp.where(p1, before, 0.0), axis=0, keepdims=True)
    r2 = jnp.sum(jnp.where(p2, before, 0.0), axis=0, keepdims=True)
    seen_ref[...] += jnp.sum(member, axis=1, keepdims=True)
    cnt_ref[...] = seen_ref[...]
    rows = jnp.concatenate([i1.astype(F32), i2.astype(F32), r1, r2, 1.0 / den, e2 / den,
                            jnp.zeros((LANES - 6, tm), F32)], axis=0)
    info_ref[...] = rows.T


def _router(x2, w3):
    n, d = x2.shape
    tm = ROW_TILE
    return pl.pallas_call(
        _router_kernel,
        grid=(n // tm,),
        in_specs=[pl.BlockSpec((tm, d), lambda i: (i, 0)),
                  pl.BlockSpec(w3.shape, lambda i: (0, 0))],
        out_specs=[pl.BlockSpec((tm, LANES), lambda i: (i, 0)),
                   pl.BlockSpec((EXPERT_ROWS, LANES), lambda i: (0, 0))],
        out_shape=[jax.ShapeDtypeStruct((n, LANES), F32),
                   jax.ShapeDtypeStruct((EXPERT_ROWS, LANES), F32)],
        scratch_shapes=[pltpu.VMEM((EXPERT_ROWS, LANES), F32)],
        compiler_params=_params(("arbitrary",)),
        name="router",
    )(x2, w3)


def _route_plan(info, counts, n_tiles):
    tmf = MOE_ROW_TILE
    cnt = counts[:N_EXPERTS, 0].astype(jnp.int32)
    padded = (cnt + tmf - 1) // tmf * tmf
    ends = jnp.cumsum(padded)
    offs = ends - padded
    eid = info[:, 0:2].astype(jnp.int32)
    rank = info[:, 2:4].astype(jnp.int32)
    pos = rank + jnp.sum(jnp.where(eid[..., None] == jnp.arange(N_EXPERTS), offs, 0), axis=-1)
    n_used = ends[-1] // tmf
    tile = jnp.arange(n_tiles)
    owner = jnp.minimum(jnp.sum(tile[:, None] * tmf >= ends[None, :], axis=1), N_EXPERTS - 1)
    owner = jnp.where(tile < n_used, owner, owner[n_used - 1])
    return (pos.reshape(-1, 1, 2 * ROW_TILE), owner.astype(jnp.int32),
            n_used.reshape(1).astype(jnp.int32))


def _to_token_major(dst_ref, val):
    tm = val.shape[0]
    for s in range(ROW_SLABS):
        dst_ref[pl.ds(s, tm, stride=ROW_SLABS), :] = val[:, s * LANES:(s + 1) * LANES]


def _from_token_major(src_ref):
    tm = src_ref.shape[0] // ROW_SLABS
    return jnp.concatenate([src_ref[pl.ds(s, tm, stride=ROW_SLABS), :] for s in range(ROW_SLABS)],
                           axis=1)


def _token_rows(ref, t):
    return ref.at[pl.ds(pl.multiple_of(t * ROW_SLABS, ROW_SLABS), ROW_SLABS), :]


def _dispatch_kernel(pos_ref, x_ref, zero_ref, xs_ref, xt_ref, sems):
    del zero_ref
    tm = x_ref.shape[0]
    i = pl.program_id(0)
    slot = lax.rem(i, 2)
    _to_token_major(xt_ref.at[slot], x_ref[...])

    def row_copy(t, s):
        return pltpu.make_async_copy(_token_rows(xt_ref.at[slot], t),
                                     _token_rows(xs_ref, pos_ref[0, 2 * t + s]), sems.at[slot])

    def issue(t, c):
        row_copy(t, 0).start()
        row_copy(t, 1).start(priority=1)
        return c

    lax.fori_loop(0, tm, issue, 0, unroll=8)

    def drain(sl):
        for _ in range(2):
            pltpu.make_async_copy(xt_ref.at[sl], xs_ref.at[pl.ds(0, tm * ROW_SLABS), :],
                                  sems.at[sl]).wait()

    @pl.when(i > 0)
    def _():
        drain(1 - slot)

    @pl.when(i == pl.num_programs(0) - 1)
    def _():
        drain(slot)


def _dispatch(x2, pos, n_rows):
    n, d = x2.shape
    tm = ROW_TILE
    return pl.pallas_call(
        _dispatch_kernel,
        grid=(n // tm,),
        in_specs=[pl.BlockSpec((None, 1, 2 * tm), lambda i: (i, 0, 0), memory_space=pltpu.SMEM),
                  pl.BlockSpec((tm, d), lambda i: (i, 0)),
                  pl.BlockSpec(memory_space=pl.ANY)],
        out_specs=pl.BlockSpec(memory_space=pl.ANY),
        out_shape=jax.ShapeDtypeStruct((n_rows * ROW_SLABS, LANES), F32),
        input_output_aliases={2: 0},
        scratch_shapes=[pltpu.VMEM((2, tm * ROW_SLABS, LANES), F32), pltpu.SemaphoreType.DMA((2,))],
        compiler_params=_params(("arbitrary",)),
        name="moe_dispatch",
    )(pos, x2, jnp.zeros((n_rows * ROW_SLABS, LANES), F32))


def _moe_ffn_kernel(owner_ref, used_ref, xs_ref, wg_ref, wu_ref, wd_ref, ys_ref, acc_ref, xb_ref):
    del owner_ref
    i, f = pl.program_id(0), pl.program_id(1)
    used = i < used_ref[0]

    @pl.when(f == 0)
    def _():
        acc_ref[...] = jnp.zeros_like(acc_ref)
        xb_ref[...] = _from_token_major(xs_ref).astype(BF16)

    @pl.when(used)
    def _():
        acc_ref[...] += _swiglu_chunk(xb_ref[...], wg_ref, wu_ref, wd_ref)

    @pl.when(f == pl.num_programs(1) - 1)
    def _():
        _to_token_major(ys_ref, acc_ref[...])


def _moe_ffn(xs, owner, n_used, layer, wg, wu, wd):
    d = wg.shape[2]
    n_rows = xs.shape[0] // ROW_SLABS
    tm, tf = MOE_ROW_TILE, FFN_COL_TILE
    ff = wg.shape[3]
    nf = ff // tf
    col = lambda i, f, used: jnp.where(i < used[0], f, nf - 1)
    return pl.pallas_call(
        _moe_ffn_kernel,
        grid_spec=pltpu.PrefetchScalarGridSpec(
            num_scalar_prefetch=2,
            grid=(n_rows // tm, nf),
            in_specs=[
                pl.BlockSpec((tm * ROW_SLABS, LANES),
                             lambda i, f, own, used: (jnp.minimum(i, used[0] - 1), 0)),
                pl.BlockSpec((None, None, d, tf),
                             lambda i, f, own, used: (layer, own[i], 0, col(i, f, used))),
                pl.BlockSpec((None, None, d, tf),
                             lambda i, f, own, used: (layer, own[i], 0, col(i, f, used))),
                pl.BlockSpec((None, None, tf, d),
                             lambda i, f, own, used: (layer, own[i], col(i, f, used), 0)),
            ],
            out_specs=pl.BlockSpec((tm * ROW_SLABS, LANES), lambda i, f, own, used: (i, 0)),
            scratch_shapes=[pltpu.VMEM((tm, d), F32), pltpu.VMEM((tm, d), BF16)],
        ),
        out_shape=jax.ShapeDtypeStruct((n_rows * ROW_SLABS, LANES), F32),
        compiler_params=_params(("arbitrary", "arbitrary")),
        name="moe_ffn",
    )(owner, n_used, xs, wg, wu, wd)


def _combine_ln_kernel(pos_ref, pos_next_ref, x_ref, info_ref, ys_ref, g_ref, b_ref, o_ref,
                       ybuf_ref, sems):
    tm = x_ref.shape[0]
    i = pl.program_id(0)
    slot = lax.rem(i, 2)

    def gather(p_ref, sl):
        def row_copy(t, s):
            return pltpu.make_async_copy(_token_rows(ys_ref, p_ref[0, 2 * t + s]),
                                         _token_rows(ybuf_ref.at[sl, s], t), sems.at[sl])

        def issue(t, c):
            row_copy(t, 0).start()
            row_copy(t, 1).start(priority=1)
            return c

        lax.fori_loop(0, tm, issue, 0, unroll=8)

    @pl.when(i == 0)
    def _():
        gather(pos_ref, 0)

    @pl.when(i + 1 < pl.num_programs(0))
    def _():
        gather(pos_next_ref, 1 - slot)

    for s in range(2):
        pltpu.make_async_copy(ys_ref.at[pl.ds(0, tm * ROW_SLABS), :], ybuf_ref.at[slot, s],
                              sems.at[slot]).wait()
    info = info_ref[...]
    lane = lax.broadcasted_iota(jnp.int32, info.shape, 1)
    w1 = jnp.sum(jnp.where(lane == 4, info, 0.0), axis=1, keepdims=True)
    w2 = jnp.sum(jnp.where(lane == 5, info, 0.0), axis=1, keepdims=True)
    y = (w1 * _from_token_major(ybuf_ref.at[slot, 0]) + w2 * _from_token_major(ybuf_ref.at[slot, 1]))
    o_ref[...] = _ln(ALPHA * x_ref[...] + y, g_ref[...], b_ref[...])


def _combine_ln(x2, info, pos, ys, gamma, beta):
    n, d = x2.shape
    tm = ROW_TILE
    last = n // tm - 1
    return pl.pallas_call(
        _combine_ln_kernel,
        grid=(n // tm,),
        in_specs=[pl.BlockSpec((None, 1, 2 * tm), lambda i: (i, 0, 0), memory_space=pltpu.SMEM),
                  pl.BlockSpec((None, 1, 2 * tm), lambda i: (jnp.minimum(i + 1, last), 0, 0),
                               memory_space=pltpu.SMEM),
                  pl.BlockSpec((tm, d), lambda i: (i, 0)),
                  pl.BlockSpec((tm, LANES), lambda i: (i, 0)),
                  pl.BlockSpec(memory_space=pl.ANY),
                  pl.BlockSpec((1, d), lambda i: (0, 0)),
                  pl.BlockSpec((1, d), lambda i: (0, 0))],
        out_specs=pl.BlockSpec((tm, d), lambda i: (i, 0)),
        out_shape=jax.ShapeDtypeStruct((n, d), F32),
        scratch_shapes=[pltpu.VMEM((2, 2, tm * ROW_SLABS, LANES), F32),
                        pltpu.SemaphoreType.DMA((2,))],
        compiler_params=_params(("arbitrary",)),
        name="moe_combine_ln",
    )(pos, pos, x2, info, ys, gamma, beta)


def _moe_ln(x2, w_router3, layer, wg, wu, wd, gamma, beta):
    n = x2.shape[0]
    n_tiles = 2 * n // MOE_ROW_TILE + N_EXPERTS
    info, counts = _router(x2, w_router3)
    pos, owner, n_used = _route_plan(info, counts, n_tiles)
    xs = _dispatch(x2, pos, n_tiles * MOE_ROW_TILE)
    ys = _moe_ffn(xs, owner, n_used, layer, wg, wu, wd)
    return _combine_ln(x2, info, pos, ys, gamma, beta)


def _pad_cols(w, width):
    return jnp.pad(w, ((0, 0), (0, width - w.shape[1])))


def _even_weights(w_in, b_forget):
    wq, wk, wv, wf, wu = jnp.split(
        w_in, [FOX_WIDTH, 2 * FOX_WIDTH, 3 * FOX_WIDTH, 3 * FOX_WIDTH + FOX_HEADS], axis=1)
    w_qk = jnp.concatenate([wq, wk], axis=1).astype(BF16)
    w_vt = wv.T.astype(BF16)
    w_f3 = _pad_cols(jnp.concatenate([wf, wf, wf], axis=1), LANES).astype(BF16)
    b_f3 = _pad_cols(jnp.concatenate([b_forget, b_forget, b_forget])[None, :], LANES)
    return w_qk, w_vt, w_f3, b_f3, wu.astype(BF16)


def _odd_weights(w_in):
    wq, wk, wv = jnp.split(w_in, 3, axis=1)
    return jnp.concatenate([wq, wk], axis=1).astype(BF16), wv.T.astype(BF16)


def _router_weights(w_router):
    return jnp.concatenate([_pad_cols(p.astype(F32), EXPERT_ROWS).astype(BF16).T
                            for p in _split3(w_router)], axis=0)


def kernel(x, ln1_g, ln1_b, ln2_g, ln2_b, ev_w_in, ev_b_forget, ev_pool_w, ev_pool_scale, ev_w_out,
           ev_ffn_gate, ev_ffn_up, ev_ffn_down, od_w_in, od_w_out, od_router, od_exp_gate, od_exp_up,
           od_exp_down):
    bsz, seq, d = x.shape
    n = bsz * seq
    x2 = x.reshape(n, d)
    for layer in range(DEPTH):
        li = layer // 2
        g1, b1 = ln1_g[layer][None, :], ln1_b[layer][None, :]
        g2, b2 = ln2_g[layer][None, :], ln2_b[layer][None, :]
        if layer % 2 == 0:
            w_qk, w_vt, w_f3, b_f3, w_u = _even_weights(ev_w_in[li], ev_b_forget[li])
            qk, vt, z, u = _proj(x2, seq, [(w_qk, "qk", BF16), (w_vt, "nt", BF16),
                                           (w_f3, "nn", F32), (w_u, "nn", F32)])
            cum3 = _fox_cum(z.reshape(bsz, seq, LANES), b_f3)
            att = _fox_attn(qk.reshape(bsz, seq, -1), cum3, vt)
            pooled = _pool(u.reshape(bsz, seq, POOL_WIDTH), ev_pool_w[li].astype(BF16),
                           ev_pool_scale[li][None, :])
            w_out = ev_w_out[li].astype(BF16)
            x2 = _outproj_ln([att.reshape(n, FOX_WIDTH), pooled.reshape(n, POOL_WIDTH)],
                             [w_out[:FOX_WIDTH], w_out[FOX_WIDTH:]], x2, g1, b1)
            x2 = _ffn_ln(x2, ev_ffn_gate[li].astype(BF16), ev_ffn_up[li].astype(BF16),
                         ev_ffn_down[li].astype(BF16), g2, b2)
        else:
            w_qk, w_vt = _odd_weights(od_w_in[li])
            qk, vt = _proj(x2, seq, [(w_qk, "qk", BF16), (w_vt, "nt", BF16)])
            att = _moba_attn(qk.reshape(bsz, seq, -1), vt)
            x2 = _outproj_ln([att.reshape(n, MOBA_WIDTH)], [od_w_out[li].astype(BF16)], x2, g1, b1)
            x2 = _moe_ln(x2, _router_weights(od_router[li]), li, od_exp_gate, od_exp_up, od_exp_down,
                         g2, b2)
    return x2.reshape(bsz, seq, d)
```
